```python
import math
import jax, jax.numpy as jnp
from jax import lax
import numpy as np

D_MODEL = 2048
BATCH = 16
SEQ = 2048
DEPTH = 4
DEC_BATCH = 8
DEC_SEQ = 32
PAST_LEN = 2048

CHUNK = 64
Q_BLOCK = 128
V_HEAD = 128
N_HEADS = D_MODEL // V_HEAD
QK_NOPE = 128
QK_ROPE = 64
Q_LORA = 768
KV_LORA = 512
ROPE_BASE = 10000.0
MLA_SCALE = 1.0 / math.sqrt(QK_NOPE + QK_ROPE)
CONV_DIM = D_MODEL
CONV_WIDTH = 3
N_MEM = 256
MEM_HEADS = 4
MEM_HEAD_DIM = 128
MEM_SCALE = 1.0 / math.sqrt(MEM_HEAD_DIM)
D_FF = 5632
EPS = 1e-6
IN_SPLITS = (Q_LORA, KV_LORA, QK_ROPE, CONV_DIM, CONV_DIM, CONV_DIM, N_HEADS * V_HEAD, CONV_DIM)
N_IN = Q_LORA + KV_LORA + QK_ROPE + 3 * CONV_DIM + N_HEADS * V_HEAD + CONV_DIM

kernel_name = "hybrid_mla_shortconv_streaming_step"


def _rms_norm(x, g):
    xf = x.astype(jnp.float32)
    y = xf * lax.rsqrt(jnp.mean(xf * xf, axis=-1, keepdims=True) + EPS)
    return (y * g.astype(jnp.float32)).astype(x.dtype)


def _rope(x, pos):
    half = QK_ROPE // 2
    inv = ROPE_BASE ** (-jnp.arange(half, dtype=jnp.float32) / half)
    ang = pos.astype(jnp.float32)[:, None] * inv[None, :]
    shape = (ang.shape[0],) + (1,) * (x.ndim - 3) + (half,)
    cos = jnp.cos(ang).reshape(shape)
    sin = jnp.sin(ang).reshape(shape)
    xf = x.astype(jnp.float32)
    x1, x2 = xf[..., :half], xf[..., half:]
    return jnp.concatenate([x1 * cos - x2 * sin, x1 * sin + x2 * cos], axis=-1).astype(x.dtype)


def _swiglu(h, w_gate, w_up, w_down):
    return (jax.nn.silu(h @ w_gate) * (h @ w_up)) @ w_down


def _attend(q_nope, q_rope, k_nope, k_rope, v, mask):
    s = (jnp.einsum('bqhd,bkhd->bhqk', q_nope, k_nope)
         + jnp.einsum('bqhr,bkr->bhqk', q_rope, k_rope)).astype(jnp.float32) * MLA_SCALE
    s = jnp.where(mask[None, None], s, -jnp.inf)
    p = jax.nn.softmax(s, axis=-1).astype(v.dtype)
    return jnp.einsum('bhqk,bkhd->bqhd', p, v)


def _mla_attention(q_nope, q_rope, k_nope, k_rope, v, q_pos, k_pos):
    B, T, H, _ = q_nope.shape
    k_chunk = k_pos // CHUNK
    if T > Q_BLOCK and T % Q_BLOCK == 0:
        def block(i):
            start = i * Q_BLOCK
            qn = lax.dynamic_slice_in_dim(q_nope, start, Q_BLOCK, axis=1)
            qr = lax.dynamic_slice_in_dim(q_rope, start, Q_BLOCK, axis=1)
            qc = lax.dynamic_slice_in_dim(q_pos, start, Q_BLOCK) // CHUNK
            return _attend(qn, qr, k_nope, k_rope, v, k_chunk[None, :] <= qc[:, None])
        out = lax.map(block, jnp.arange(T // Q_BLOCK))
        return jnp.moveaxis(out, 0, 1).reshape(B, T, H * V_HEAD)
    mask = k_chunk[None, :] <= (q_pos // CHUNK)[:, None]
    return _attend(q_nope, q_rope, k_nope, k_rope, v, mask).reshape(B, T, H * V_HEAD)


def _short_conv(u_b, u_c, u_x, conv_w, prev_rows):
    T = u_x.shape[1]
    z = jnp.concatenate([prev_rows, u_c * u_x], axis=1)
    y = conv_w[0] * z[:, 0:T]
    for k in range(1, CONV_WIDTH):
        y = y + conv_w[k] * z[:, k:k + T]
    return u_b * y, z[:, -(CONV_WIDTH - 1):]


def _mem_kv(mem, mem_norm, w_ck, w_cv, ck_norm):
    B, M, _ = mem.shape
    m = _rms_norm(mem, mem_norm)
    k = _rms_norm((m @ w_ck).reshape(B, M, MEM_HEADS, MEM_HEAD_DIM), ck_norm)
    v = (m @ w_cv).reshape(B, M, MEM_HEADS, MEM_HEAD_DIM)
    return k, v


def _cross(h, mem_k, mem_v, w_cq, cq_norm, w_co):
    B, T, _ = h.shape
    q = _rms_norm((h @ w_cq).reshape(B, T, MEM_HEADS, MEM_HEAD_DIM), cq_norm)
    s = jnp.einsum('bqhd,bkhd->bhqk', q, mem_k).astype(jnp.float32) * MEM_SCALE
    p = jax.nn.softmax(s, axis=-1).astype(mem_v.dtype)
    o = jnp.einsum('bhqk,bkhd->bqhd', p, mem_v).reshape(B, T, MEM_HEADS * MEM_HEAD_DIM)
    return o @ w_co


def _layer(x, past_lat, past_krope, conv_prev, mem_k, mem_v, p):
    B, T, _ = x.shape
    P = past_lat.shape[1]
    q_pos = P + jnp.arange(T, dtype=jnp.int32)
    k_pos = jnp.arange(P + T, dtype=jnp.int32)
    x = x + 0.5 * _swiglu(_rms_norm(x, p['ffn1_norm']), p['ffn1_w_gate'], p['ffn1_w_up'], p['ffn1_w_down'])
    h = _rms_norm(x, p['mix_norm'])
    offsets = np.cumsum(IN_SPLITS)[:-1].tolist()
    c_q, c_kv, k_r, u_b, u_c, u_x, g_a, g_c = jnp.split(h @ p['w_in'], offsets, axis=-1)
    q = (_rms_norm(c_q, p['q_a_norm']) @ p['w_uq']).reshape(B, T, N_HEADS, QK_NOPE + QK_ROPE)
    q_nope = _rms_norm(q[..., :QK_NOPE], p['q_nope_norm'])
    q_rope = _rope(_rms_norm(q[..., QK_NOPE:], p['q_rope_norm']), q_pos)
    new_lat = _rms_norm(c_kv, p['kv_a_norm'])
    new_krope = _rope(_rms_norm(k_r, p['k_rope_norm']), q_pos)
    lat = jnp.concatenate([past_lat, new_lat], axis=1)
    krope = jnp.concatenate([past_krope, new_krope], axis=1)
    kv = (lat @ p['w_ukv']).reshape(B, P + T, N_HEADS, QK_NOPE + V_HEAD)
    k_nope = _rms_norm(kv[..., :QK_NOPE], p['k_nope_norm'])
    v = kv[..., QK_NOPE:]
    attn = _mla_attention(q_nope, q_rope, k_nope, krope, v, q_pos, k_pos)
    conv_out, new_conv = _short_conv(u_b, u_c, u_x, p['conv_w'], conv_prev)
    mixed = jax.nn.sigmoid(g_a) * attn + jax.nn.sigmoid(g_c) * conv_out
    x = x + mixed @ p['w_o']
    x = x + _cross(_rms_norm(x, p['cross_norm']), mem_k, mem_v, p['w_cq'], p['cq_norm'], p['w_co'])
    x = x + 0.5 * _swiglu(_rms_norm(x, p['ffn2_norm']), p['ffn2_w_gate'], p['ffn2_w_up'], p['ffn2_w_down'])
    return x, new_lat, new_krope, new_conv


def setup_inputs(seed: int = 0) -> dict:
    key = jax.random.key(seed)
    ks = list(jax.random.split(key, 48))
    f32 = jnp.float32

    def nrm(shape, scale=1.0):
        return scale * jax.random.normal(ks.pop(), shape, f32)

    def gain(n):
        return 1.0 + 0.01 * jax.random.normal(ks.pop(), (DEPTH, n), f32)

    L = DEPTH
    return {
        'x_prompt': nrm((BATCH, SEQ, D_MODEL)),
        'x_sample': nrm((DEC_BATCH, DEC_SEQ, D_MODEL)),
        'mem_prompt': nrm((BATCH, N_MEM, D_MODEL)),
        'cache_kv_latent': nrm((L, DEC_BATCH, PAST_LEN, KV_LORA)),
        'cache_k_rope': nrm((L, DEC_BATCH, PAST_LEN, QK_ROPE)),
        'state_conv': nrm((L, DEC_BATCH, CONV_WIDTH - 1, CONV_DIM)),
        'cache_mem_k': nrm((L, DEC_BATCH, N_MEM, MEM_HEADS, MEM_HEAD_DIM)),
        'cache_mem_v': nrm((L, DEC_BATCH, N_MEM, MEM_HEADS, MEM_HEAD_DIM)),
        'ffn1_norm': gain(D_MODEL),
        'ffn1_w_gate': nrm((L, D_MODEL, D_FF), D_MODEL ** -0.5),
        'ffn1_w_up': nrm((L, D_MODEL, D_FF), D_MODEL ** -0.5),
        'ffn1_w_down': nrm((L, D_FF, D_MODEL), D_FF ** -0.5),
        'mix_norm': gain(D_MODEL),
        'w_in': nrm((L, D_MODEL, N_IN), D_MODEL ** -0.5),
        'q_a_norm': gain(Q_LORA),
        'w_uq': nrm((L, Q_LORA, N_HEADS * (QK_NOPE + QK_ROPE)), Q_LORA ** -0.5),
        'q_nope_norm': gain(QK_NOPE),
        'q_rope_norm': gain(QK_ROPE),
        'kv_a_norm': gain(KV_LORA),
        'w_ukv': nrm((L, KV_LORA, N_HEADS * (QK_NOPE + V_HEAD)), KV_LORA ** -0.5),
        'k_nope_norm': gain(QK_NOPE),
        'k_rope_norm': gain(QK_ROPE),
        'conv_w': nrm((L, CONV_WIDTH, CONV_DIM), CONV_WIDTH ** -0.5),
        'w_o': nrm((L, D_MODEL, D_MODEL), D_MODEL ** -0.5),
        'cross_norm': gain(D_MODEL),
        'mem_norm': gain(D_MODEL),
        'w_cq': nrm((L, D_MODEL, MEM_HEADS * MEM_HEAD_DIM), D_MODEL ** -0.5),
        'w_ck': nrm((L, D_MODEL, MEM_HEADS * MEM_HEAD_DIM), D_MODEL ** -0.5),
        'w_cv': nrm((L, D_MODEL, MEM_HEADS * MEM_HEAD_DIM), D_MODEL ** -0.5),
        'cq_norm': gain(MEM_HEAD_DIM),
        'ck_norm': gain(MEM_HEAD_DIM),
        'w_co': nrm((L, MEM_HEADS * MEM_HEAD_DIM, D_MODEL), (MEM_HEADS * MEM_HEAD_DIM) ** -0.5),
        'ffn2_norm': gain(D_MODEL),
        'ffn2_w_gate': nrm((L, D_MODEL, D_FF), D_MODEL ** -0.5),
        'ffn2_w_up': nrm((L, D_MODEL, D_FF), D_MODEL ** -0.5),
        'ffn2_w_down': nrm((L, D_FF, D_MODEL), D_FF ** -0.5),
    }


def reference(x_prompt, x_sample, mem_prompt, cache_kv_latent, cache_k_rope, state_conv,
              cache_mem_k, cache_mem_v, ffn1_norm, ffn1_w_gate, ffn1_w_up, ffn1_w_down,
              mix_norm, w_in, q_a_norm, w_uq, q_nope_norm, q_rope_norm, kv_a_norm, w_ukv,
              k_nope_norm, k_rope_norm, conv_w, w_o, cross_norm, mem_norm, w_cq, w_ck, w_cv,
              cq_norm, ck_norm, w_co, ffn2_norm, ffn2_w_gate, ffn2_w_up, ffn2_w_down):
    Bp = x_prompt.shape[0]
    dt = x_prompt.dtype
    empty_lat = jnp.zeros((Bp, 0, KV_LORA), dt)
    empty_krope = jnp.zeros((Bp, 0, QK_ROPE), dt)
    zero_conv = jnp.zeros((Bp, CONV_WIDTH - 1, CONV_DIM), dt)
    yp, ys = x_prompt, x_sample
    lat_p, kr_p, cv_p, mk_p, mv_p = [], [], [], [], []
    lat_s, kr_s, cv_s = [], [], []
    for l in range(DEPTH):
        p = {
            'ffn1_norm': ffn1_norm[l], 'ffn1_w_gate': ffn1_w_gate[l], 'ffn1_w_up': ffn1_w_up[l],
            'ffn1_w_down': ffn1_w_down[l], 'mix_norm': mix_norm[l], 'w_in': w_in[l],
            'q_a_norm': q_a_norm[l], 'w_uq': w_uq[l], 'q_nope_norm': q_nope_norm[l],
            'q_rope_norm': q_rope_norm[l], 'kv_a_norm': kv_a_norm[l], 'w_ukv': w_ukv[l],
            'k_nope_norm': k_nope_norm[l], 'k_rope_norm': k_rope_norm[l], 'conv_w': conv_w[l],
            'w_o': w_o[l], 'cross_norm': cross_norm[l], 'w_cq': w_cq[l], 'cq_norm': cq_norm[l],
            'w_co': w_co[l], 'ffn2_norm': ffn2_norm[l], 'ffn2_w_gate': ffn2_w_gate[l],
            'ffn2_w_up': ffn2_w_up[l], 'ffn2_w_down': ffn2_w_down[l],
        }
        mk, mv = _mem_kv(mem_prompt, mem_norm[l], w_ck[l], w_cv[l], ck_norm[l])
        yp, nl, nk, nc = _layer(yp, empty_lat, empty_krope, zero_conv, mk, mv, p)
        lat_p.append(nl); kr_p.append(nk); cv_p.append(nc); mk_p.append(mk); mv_p.append(mv)
        ys, nl, nk, nc = _layer(ys, cache_kv_latent[l], cache_k_rope[l], state_conv[l],
                                cache_mem_k[l], cache_mem_v[l], p)
        lat_s.append(nl); kr_s.append(nk); cv_s.append(nc)
    return (yp, ys, jnp.stack(lat_p), jnp.stack(kr_p), jnp.stack(cv_p), jnp.stack(mk_p),
            jnp.stack(mv_p), jnp.stack(lat_s), jnp.stack(kr_s), jnp.stack(cv_s))
```

```python
import functools
import math

import jax
import jax.numpy as jnp
from jax import lax
from jax.experimental import pallas as pl
from jax.experimental.pallas import tpu as pltpu

F32 = jnp.float32
BF16 = jnp.bfloat16

V_HEAD = 128
QK_NOPE = 128
QK_ROPE = 64
ROPE_HALF = QK_ROPE // 2
QK_DIM = QK_NOPE + QK_ROPE
Q_LORA = 768
KV_LORA = 512
CHUNK = 64
ROPE_BASE = 10000.0
MEM_HEADS = 4
MEM_HEAD_DIM = 128
CONV_WIDTH = 3
EPS = 1e-6
MLA_SCALE = 1.0 / math.sqrt(QK_NOPE + QK_ROPE)
MEM_SCALE = 1.0 / math.sqrt(MEM_HEAD_DIM)

LANE = 128
V7X_VMEM_BYTES = 64 * 1024 * 1024
VMEM_LIMIT_BYTES = V7X_VMEM_BYTES * 13 // 16

SMALL_W = Q_LORA + KV_LORA + LANE


def _params(*sem):
    return pltpu.CompilerParams(dimension_semantics=sem, vmem_limit_bytes=VMEM_LIMIT_BYTES)


def _tile(n, target):
    t = min(n, target)
    while n % t:
        t -= 1
    return t


def _rms(x, g):
    ms = jnp.mean(x * x, axis=-1, keepdims=True)
    return x * lax.rsqrt(ms + EPS) * g


def _sigmoid(x):
    return 1.0 / (1.0 + jnp.exp(-x))


def _dot(a, b):
    return jnp.dot(a, b, preferred_element_type=F32)


def _dot_nt(a, b):
    return lax.dot_general(a, b, (((1,), (1,)), ((), ())), preferred_element_type=F32)


def _ffn_body(x_ref, g_ref, wg_ref, wu_ref, wd_ref, o_ref, h_ref):
    @pl.when(pl.program_id(1) == 0)
    def _():
        x = x_ref[...]
        h_ref[...] = _rms(x, g_ref[...]).astype(BF16)
        o_ref[...] = x

    h = h_ref[...]
    gate = _dot(h, wg_ref[...])
    up = _dot(h, wu_ref[...])
    act = (gate * _sigmoid(gate) * up * 0.5).astype(BF16)
    o_ref[...] += _dot(act, wd_ref[...])


def _ffn(x, g, wg, wu, wd, layer):
    n, d = x.shape
    f = wg.shape[-1]
    tm, tf = _tile(n, 512), _tile(f, 512)
    return pl.pallas_call(
        _ffn_body,
        grid=(n // tm, f // tf),
        in_specs=[
            pl.BlockSpec((tm, d), lambda i, j: (i, 0)),
            pl.BlockSpec((None, 1, d), lambda i, j: (layer, 0, 0)),
            pl.BlockSpec((None, d, tf), lambda i, j: (layer, 0, j)),
            pl.BlockSpec((None, d, tf), lambda i, j: (layer, 0, j)),
            pl.BlockSpec((None, tf, d), lambda i, j: (layer, j, 0)),
        ],
        out_specs=pl.BlockSpec((tm, d), lambda i, j: (i, 0)),
        out_shape=jax.ShapeDtypeStruct((n, d), F32),
        scratch_shapes=[pltpu.VMEM((tm, d), BF16)],
        compiler_params=_params("parallel", "arbitrary"),
        name="ffn",
    )(x, g, wg, wu, wd)


def _mla_proj_body(x_ref, gmix_ref, ws_ref, gqa_ref, wuq_ref, gqn_ref, gqr_ref, gkva_ref, gkr_ref,
                   cos_ref, sin_ref, q_ref, lat_ref, kr_ref, krt_ref, *, n_heads):
    tm = x_ref.shape[0]
    h = _rms(x_ref[...], gmix_ref[...]).astype(BF16)
    u = _dot(h, ws_ref[...])
    c_q = u[:, :Q_LORA]
    c_kv = u[:, Q_LORA:Q_LORA + KV_LORA]
    k_r = u[:, Q_LORA + KV_LORA:]

    lat_ref[...] = _rms(c_kv, gkva_ref[...])

    cos_t = cos_ref[...]
    sin_t = sin_ref[...]
    lane = lax.broadcasted_iota(jnp.int32, (tm, LANE), 1)
    first_half = (lane & ROPE_HALF) == 0
    low_group = lane < QK_ROPE

    def rope(y):
        swapped = jnp.where(first_half, pltpu.roll(y, LANE - ROPE_HALF, 1), pltpu.roll(y, ROPE_HALF, 1))
        return y * cos_t + swapped * sin_t

    ms = jnp.sum(k_r * k_r, axis=-1, keepdims=True) * (1.0 / QK_ROPE)
    kr = rope(k_r * lax.rsqrt(ms + EPS) * gkr_ref[...])
    kr_ref[...] = kr[:, :QK_ROPE]
    krt_ref[...] = kr.T[:QK_ROPE, :].astype(BF16)

    cq_n = _rms(c_q, gqa_ref[...]).astype(BF16)
    q = _dot(cq_n, wuq_ref[...])
    gqn = gqn_ref[...]
    for hd in range(n_heads):
        qn = _rms(q[:, hd * QK_NOPE:(hd + 1) * QK_NOPE], gqn) * MLA_SCALE
        q_ref[hd, :, :QK_NOPE] = qn.astype(BF16)
    rope_base = n_heads * QK_NOPE
    gqr = gqr_ref[...]
    for pair in range(n_heads // 2):
        col = q[:, rope_base + pair * LANE: rope_base + (pair + 1) * LANE]
        sq = col * col
        lo = jnp.sum(jnp.where(low_group, sq, 0.0), axis=-1, keepdims=True)
        hi = jnp.sum(jnp.where(low_group, 0.0, sq), axis=-1, keepdims=True)
        ms = jnp.where(low_group, lo, hi) * (1.0 / QK_ROPE)
        y = (rope(col * lax.rsqrt(ms + EPS) * gqr) * MLA_SCALE).astype(BF16)
        q_ref[2 * pair, :, QK_NOPE:] = y[:, :QK_ROPE]
        q_ref[2 * pair + 1, :, QK_NOPE:] = y[:, QK_ROPE:]


def _mla_proj(x, gmix, ws, gqa, wuq, gqn, gqr2, gkva, gkr2, cos_t, sin_t, layer, tm):
    n, d = x.shape
    n_heads = wuq.shape[-1] // QK_DIM
    n_pos_tiles = cos_t.shape[0] // tm
    const = dict(pipeline_mode=pl.Buffered(1))
    return pl.pallas_call(
        functools.partial(_mla_proj_body, n_heads=n_heads),
        grid=(n // tm,),
        in_specs=[
            pl.BlockSpec((tm, d), lambda i: (i, 0)),
            pl.BlockSpec((None, 1, d), lambda i: (layer, 0, 0)),
            pl.BlockSpec((None, d, SMALL_W), lambda i: (layer, 0, 0), **const),
            pl.BlockSpec((None, 1, Q_LORA), lambda i: (layer, 0, 0)),
            pl.BlockSpec((None, Q_LORA, n_heads * QK_DIM), lambda i: (layer, 0, 0), **const),
            pl.BlockSpec((None, 1, QK_NOPE), lambda i: (layer, 0, 0)),
            pl.BlockSpec((None, 1, LANE), lambda i: (layer, 0, 0)),
            pl.BlockSpec((None, 1, KV_LORA), lambda i: (layer, 0, 0)),
            pl.BlockSpec((None, 1, LANE), lambda i: (layer, 0, 0)),
            pl.BlockSpec((tm, LANE), lambda i: (i % n_pos_tiles, 0)),
            pl.BlockSpec((tm, LANE), lambda i: (i % n_pos_tiles, 0)),
        ],
        out_specs=[
            pl.BlockSpec((n_heads, tm, QK_DIM), lambda i: (0, i, 0)),
            pl.BlockSpec((tm, KV_LORA), lambda i: (i, 0)),
            pl.BlockSpec((tm, QK_ROPE), lambda i: (i, 0)),
            pl.BlockSpec((QK_ROPE, tm), lambda i: (0, i)),
        ],
        out_shape=[
            jax.ShapeDtypeStruct((n_heads, n, QK_DIM), BF16),
            jax.ShapeDtypeStruct((n, KV_LORA), F32),
            jax.ShapeDtypeStruct((n, QK_ROPE), F32),
            jax.ShapeDtypeStruct((QK_ROPE, n), BF16),
        ],
        compiler_params=_params("parallel"),
        name="mla_proj",
    )(x, gmix, ws, gqa, wuq, gqn, gqr2, gkva, gkr2, cos_t, sin_t)


def _kv_proj_body(lat_ref, krt_ref, wukt_ref, wuv_ref, gkn_ref, kt_ref, v_ref, *, n_heads):
    lat = lat_ref[...].astype(BF16)
    knt = _dot_nt(wukt_ref[...], lat)
    krt = krt_ref[...]
    gkn = gkn_ref[...]
    for hd in range(n_heads):
        blk = knt[hd * QK_NOPE:(hd + 1) * QK_NOPE, :]
        ms = jnp.mean(blk * blk, axis=0, keepdims=True)
        kt_ref[hd, :QK_NOPE, :] = (blk * lax.rsqrt(ms + EPS) * gkn).astype(BF16)
        kt_ref[hd, QK_NOPE:, :] = krt
    v = _dot(lat, wuv_ref[...])
    for hd in range(n_heads):
        v_ref[hd] = v[:, hd * V_HEAD:(hd + 1) * V_HEAD].astype(BF16)


def _kv_proj(lat, krt, wukt, wuv, gkn_col, layer):
    b, tk, _ = lat.shape
    n_heads = wuv.shape[-1] // V_HEAD
    tm = _tile(tk, 512)
    const = dict(pipeline_mode=pl.Buffered(1))
    return pl.pallas_call(
        functools.partial(_kv_proj_body, n_heads=n_heads),
        grid=(b, tk // tm),
        in_specs=[
            pl.BlockSpec((None, tm, KV_LORA), lambda i, j: (i, j, 0)),
            pl.BlockSpec((None, QK_ROPE, tm), lambda i, j: (i, 0, j)),
            pl.BlockSpec((None, n_heads * QK_NOPE, KV_LORA), lambda i, j: (layer, 0, 0), **const),
            pl.BlockSpec((None, KV_LORA, n_heads * V_HEAD), lambda i, j: (layer, 0, 0), **const),
            pl.BlockSpec((None, QK_NOPE, 1), lambda i, j: (layer, 0, 0)),
        ],
        out_specs=[
            pl.BlockSpec((None, n_heads, QK_DIM, tm), lambda i, j: (i, 0, 0, j)),
            pl.BlockSpec((None, n_heads, tm, V_HEAD), lambda i, j: (i, 0, j, 0)),
        ],
        out_shape=[
            jax.ShapeDtypeStruct((b, n_heads, QK_DIM, tk), BF16),
            jax.ShapeDtypeStruct((b, n_heads, tk, V_HEAD), BF16),
        ],
        compiler_params=_params("parallel", "parallel"),
        name="kv_proj",
    )(lat, krt, wukt, wuv, gkn_col)


def _attn_body(q_ref, kt_ref, v_ref, o_ref, *, t, tq, past, n_valid):
    tk = kt_ref.shape[-1]
    for qs in range(0, t, tq):
        c_lo = (past + qs) // CHUNK
        c_hi = (past + qs + tq - 1) // CHUNK
        full_end = min((c_lo + 1) * CHUNK, n_valid) // LANE * LANE
        edge_end = min(-(-min((c_hi + 1) * CHUNK, n_valid) // LANE) * LANE, tk)
        q = q_ref[qs:qs + tq, :]
        parts = []
        if full_end > 0:
            parts.append((_dot(q, kt_ref[:, :full_end]), 0, full_end))
        if edge_end > full_end:
            w = edge_end - full_end
            s = _dot(q, kt_ref[:, full_end:edge_end])
            q_chunk = (lax.broadcasted_iota(jnp.int32, (tq, w), 0) + (past + qs)) >> 6
            k_pos = lax.broadcasted_iota(jnp.int32, (tq, w), 1) + full_end
            ok = jnp.logical_and((k_pos >> 6) <= q_chunk, k_pos < n_valid)
            parts.append((jnp.where(ok, s, -jnp.inf), full_end, edge_end))
        m = functools.reduce(jnp.maximum, [jnp.max(s, axis=-1, keepdims=True) for s, _, _ in parts])
        l = 0.0
        o = 0.0
        for s, lo, hi in parts:
            p = jnp.exp(s - m)
            l = l + jnp.sum(p, axis=-1, keepdims=True)
            o = o + _dot(p.astype(BF16), v_ref[lo:hi, :])
        o_ref[qs:qs + tq, :] = o * (1.0 / l)


def _attention(q, kt, v, *, t, past, n_valid):
    assert CHUNK == 64
    n_heads, n, _ = q.shape
    b, _, _, tk = kt.shape
    tq = _tile(t, 256)
    return pl.pallas_call(
        functools.partial(_attn_body, t=t, tq=tq, past=past, n_valid=n_valid),
        grid=(b, n_heads),
        in_specs=[
            pl.BlockSpec((None, t, QK_DIM), lambda i, j: (j, i, 0)),
            pl.BlockSpec((None, None, QK_DIM, tk), lambda i, j: (i, j, 0, 0)),
            pl.BlockSpec((None, None, tk, V_HEAD), lambda i, j: (i, j, 0, 0)),
        ],
        out_specs=pl.BlockSpec((t, V_HEAD), lambda i, j: (i, j)),
        out_shape=jax.ShapeDtypeStruct((n, n_heads * V_HEAD), F32),
        compiler_params=_params("parallel", "parallel"),
        name="attention",
    )(q, kt, v)


def _mix_body(x_ref, g_ref, a_ref, wb_ref, wc_ref, wx_ref, wga_ref, wgc_ref, cw_ref, prev_ref, wo_ref,
              o_ref, nc_ref, h_ref, carry_ref, *, t_seq, tiles_per_seq):
    tm, tn = a_ref.shape
    i = pl.program_id(0)
    j = pl.program_id(1)

    @pl.when(j == 0)
    def _():
        x = x_ref[...]
        h_ref[...] = _rms(x, g_ref[...]).astype(BF16)
        o_ref[...] = x

    h = h_ref[...]
    z = _dot(h, wc_ref[...]) * _dot(h, wx_ref[...])
    rows = lax.broadcasted_iota(jnp.int32, (tm, tn), 0)
    z1 = pltpu.roll(z, 1, 0)
    z2 = pltpu.roll(z, 2, 0)
    if tiles_per_seq == 1:
        for s in range(tm // t_seq):
            p0 = prev_ref[s, 0:1, :]
            p1 = prev_ref[s, 1:2, :]
            z1 = jnp.where(rows == s * t_seq, p1, z1)
            z2 = jnp.where(rows == s * t_seq, p0, jnp.where(rows == s * t_seq + 1, p1, z2))
            nc_ref[s] = z[(s + 1) * t_seq - 2:(s + 1) * t_seq, :]
    else:
        @pl.when(i % tiles_per_seq == 0)
        def _():
            carry_ref[j] = prev_ref[0]

        p0 = carry_ref[j, 0:1, :]
        p1 = carry_ref[j, 1:2, :]
        z1 = jnp.where(rows == 0, p1, z1)
        z2 = jnp.where(rows == 0, p0, jnp.where(rows == 1, p1, z2))
        carry_ref[j] = z[tm - 2:, :]
        nc_ref[0] = z[tm - 2:, :]
    cw = cw_ref[...]
    y = cw[0:1, :] * z2 + cw[1:2, :] * z1 + cw[2:3, :] * z
    conv_out = _dot(h, wb_ref[...]) * y
    mixed = _sigmoid(_dot(h, wga_ref[...])) * a_ref[...] + _sigmoid(_dot(h, wgc_ref[...])) * conv_out
    o_ref[...] += _dot(mixed.astype(BF16), wo_ref[...])


def _mix(x, g, attn, wb, wc, wx, wga, wgc, cw, prev, wo, layer, t_seq, tm):
    n, d = x.shape
    tn = _tile(d, 256)
    assert CONV_WIDTH == 3 and t_seq >= CONV_WIDTH - 1
    if tm >= t_seq:
        assert tm % t_seq == 0
        n_seq, tiles_per_seq = tm // t_seq, 1
        prev_map = lambda i, j: (i, 0, j)
    else:
        assert t_seq % tm == 0
        n_seq, tiles_per_seq = 1, t_seq // tm
        prev_map = lambda i, j: (i // tiles_per_seq, 0, j)
    w_spec = pl.BlockSpec((None, d, tn), lambda i, j: (layer, 0, j))
    y, tail_rows = pl.pallas_call(
        functools.partial(_mix_body, t_seq=t_seq, tiles_per_seq=tiles_per_seq),
        grid=(n // tm, d // tn),
        in_specs=[
            pl.BlockSpec((tm, d), lambda i, j: (i, 0)),
            pl.BlockSpec((None, 1, d), lambda i, j: (layer, 0, 0)),
            pl.BlockSpec((tm, tn), lambda i, j: (i, j)),
            w_spec, w_spec, w_spec, w_spec, w_spec,
            pl.BlockSpec((None, CONV_WIDTH, tn), lambda i, j: (layer, 0, j)),
            pl.BlockSpec((n_seq, CONV_WIDTH - 1, tn), prev_map),
            pl.BlockSpec((None, tn, d), lambda i, j: (layer, j, 0)),
        ],
        out_specs=[
            pl.BlockSpec((tm, d), lambda i, j: (i, 0)),
            pl.BlockSpec((n_seq, CONV_WIDTH - 1, tn), lambda i, j: (i, 0, j)),
        ],
        out_shape=[
            jax.ShapeDtypeStruct((n, d), F32),
            jax.ShapeDtypeStruct((n // tm * n_seq, CONV_WIDTH - 1, d), F32),
        ],
        scratch_shapes=[pltpu.VMEM((tm, d), BF16), pltpu.VMEM((d // tn, CONV_WIDTH - 1, tn), F32)],
        compiler_params=_params("arbitrary", "arbitrary"),
        name="mix",
    )(x, g, attn, wb, wc, wx, wga, wgc, cw, prev, wo)
    return y, tail_rows.reshape(-1, tiles_per_seq, CONV_WIDTH - 1, d)[:, -1]


def _cross_body(x_ref, g_ref, wq_ref, gq_ref, mk_ref, mv_ref, wo_ref, o_ref, att_ref, *, t_seq):
    tm = x_ref.shape[0]
    x = x_ref[...]
    h = _rms(x, g_ref[...]).astype(BF16)
    q = _dot(h, wq_ref[...])
    gq = gq_ref[...]
    rows = min(t_seq, tm)
    for s in range(tm // rows):
        r0 = s * rows
        for hd in range(MEM_HEADS):
            c0 = hd * MEM_HEAD_DIM
            qh = (_rms(q[r0:r0 + rows, c0:c0 + MEM_HEAD_DIM], gq) * MEM_SCALE).astype(BF16)
            kh = mk_ref[s, :, c0:c0 + MEM_HEAD_DIM].astype(BF16)
            vh = mv_ref[s, :, c0:c0 + MEM_HEAD_DIM].astype(BF16)
            sc = _dot_nt(qh, kh)
            p = jnp.exp(sc - jnp.max(sc, axis=-1, keepdims=True))
            l = jnp.sum(p, axis=-1, keepdims=True)
            att_ref[r0:r0 + rows, c0:c0 + MEM_HEAD_DIM] = (_dot(p.astype(BF16), vh) * (1.0 / l)).astype(BF16)
    o_ref[...] = x + _dot(att_ref[...], wo_ref[...])


def _cross(x, g, wq, gq, mem_k, mem_v, wo, layer, t_seq, tm, mem_layer):
    n, d = x.shape
    _, _, n_mem, dm = mem_k.shape
    if tm >= t_seq:
        n_seq = tm // t_seq
        mem_map = lambda i: (mem_layer, i, 0, 0)
    else:
        n_seq, tiles_per_seq = 1, t_seq // tm
        mem_map = lambda i: (mem_layer, i // tiles_per_seq, 0, 0)
    const = dict(pipeline_mode=pl.Buffered(1))
    return pl.pallas_call(
        functools.partial(_cross_body, t_seq=t_seq),
        grid=(n // tm,),
        in_specs=[
            pl.BlockSpec((tm, d), lambda i: (i, 0)),
            pl.BlockSpec((None, 1, d), lambda i: (layer, 0, 0)),
            pl.BlockSpec((None, d, dm), lambda i: (layer, 0, 0), **const),
            pl.BlockSpec((None, 1, MEM_HEAD_DIM), lambda i: (layer, 0, 0)),
            pl.BlockSpec((None, n_seq, n_mem, dm), mem_map),
            pl.BlockSpec((None, n_seq, n_mem, dm), mem_map),
            pl.BlockSpec((None, dm, d), lambda i: (layer, 0, 0), **const),
        ],
        out_specs=pl.BlockSpec((tm, d), lambda i: (i, 0)),
        out_shape=jax.ShapeDtypeStruct((n, d), F32),
        scratch_shapes=[pltpu.VMEM((tm, dm), BF16)],
        compiler_params=_params("parallel"),
        name="cross",
    )(x, g, wq, gq, mem_k, mem_v, wo)


def _mem_kv_body(m_ref, g_ref, wk_ref, wv_ref, gk_ref, k_ref, v_ref):
    m = _rms(m_ref[...], g_ref[...]).astype(BF16)
    k = _dot(m, wk_ref[...])
    gk = gk_ref[...]
    for hd in range(MEM_HEADS):
        c0 = hd * MEM_HEAD_DIM
        k_ref[:, c0:c0 + MEM_HEAD_DIM] = _rms(k[:, c0:c0 + MEM_HEAD_DIM], gk)
    v_ref[...] = _dot(m, wv_ref[...])


def _mem_kv(mem, g, wk, wv, gk):
    n, d = mem.shape
    n_layers, _, dm = wk.shape
    tm = _tile(n, 512)
    out = jax.ShapeDtypeStruct((n_layers, n, dm), F32)
    return pl.pallas_call(
        _mem_kv_body,
        grid=(n_layers, n // tm),
        in_specs=[
            pl.BlockSpec((tm, d), lambda l, i: (i, 0)),
            pl.BlockSpec((None, 1, d), lambda l, i: (l, 0, 0)),
            pl.BlockSpec((None, d, dm), lambda l, i: (l, 0, 0)),
            pl.BlockSpec((None, d, dm), lambda l, i: (l, 0, 0)),
            pl.BlockSpec((None, 1, MEM_HEAD_DIM), lambda l, i: (l, 0, 0)),
        ],
        out_specs=[
            pl.BlockSpec((None, tm, dm), lambda l, i: (l, i, 0)),
            pl.BlockSpec((None, tm, dm), lambda l, i: (l, i, 0)),
        ],
        out_shape=[out, out],
        compiler_params=_params("parallel", "parallel"),
        name="mem_kv",
    )(mem, g, wk, wv, gk)


def _rope_tables(pos, reps):
    inv = ROPE_BASE ** (-jnp.arange(ROPE_HALF, dtype=F32) / ROPE_HALF)
    ang = pos.astype(F32)[:, None] * inv[None, :]
    cos, sin = jnp.cos(ang), jnp.sin(ang)
    groups = LANE // QK_ROPE
    cos_t = jnp.tile(jnp.concatenate([cos, cos], axis=1), (reps, groups))
    sin_t = jnp.tile(jnp.concatenate([-sin, sin], axis=1), (reps, groups))
    return cos_t, sin_t


def kernel(x_prompt, x_sample, mem_prompt, cache_kv_latent, cache_k_rope, state_conv, cache_mem_k, cache_mem_v, ffn1_norm, ffn1_w_gate, ffn1_w_up, ffn1_w_down, mix_norm, w_in, q_a_norm, w_uq, q_nope_norm, q_rope_norm, kv_a_norm, w_ukv, k_nope_norm, k_rope_norm, conv_w, w_o, cross_norm, mem_norm, w_cq, w_ck, w_cv, cq_norm, ck_norm, w_co, ffn2_norm, ffn2_w_gate, ffn2_w_up, ffn2_w_down):
    bp, tp, d = x_prompt.shape
    bs, ts, _ = x_sample.shape
    n_layers = w_in.shape[0]
    past = cache_kv_latent.shape[2]
    n_mem = mem_prompt.shape[1]
    n_heads = w_uq.shape[-1] // QK_DIM
    dm = MEM_HEADS * MEM_HEAD_DIM

    row = lambda a: a[:, None, :]
    w1g, w1u, w1d = ffn1_w_gate.astype(BF16), ffn1_w_up.astype(BF16), ffn1_w_down.astype(BF16)
    w2g, w2u, w2d = ffn2_w_gate.astype(BF16), ffn2_w_up.astype(BF16), ffn2_w_down.astype(BF16)
    n_small = Q_LORA + KV_LORA + QK_ROPE
    w_small = jnp.pad(w_in[:, :, :n_small], ((0, 0), (0, 0), (0, SMALL_W - n_small))).astype(BF16)
    w_b, w_c, w_x, w_ga, w_gc = [w_in[:, :, n_small + k * d: n_small + (k + 1) * d].astype(BF16) for k in range(5)]
    uq = w_uq.reshape(n_layers, Q_LORA, n_heads, QK_DIM)
    w_uq_p = jnp.concatenate([uq[..., :QK_NOPE].reshape(n_layers, Q_LORA, n_heads * QK_NOPE),
                              uq[..., QK_NOPE:].reshape(n_layers, Q_LORA, n_heads * QK_ROPE)], axis=-1).astype(BF16)
    ukv = w_ukv.reshape(n_layers, KV_LORA, n_heads, QK_NOPE + V_HEAD)
    w_uk_t = jnp.transpose(ukv[..., :QK_NOPE], (0, 2, 3, 1)).reshape(n_layers, n_heads * QK_NOPE, KV_LORA).astype(BF16)
    w_uv = ukv[..., QK_NOPE:].reshape(n_layers, KV_LORA, n_heads * V_HEAD).astype(BF16)
    w_o_b = w_o.astype(BF16)
    w_cq_b, w_ck_b, w_cv_b, w_co_b = w_cq.astype(BF16), w_ck.astype(BF16), w_cv.astype(BF16), w_co.astype(BF16)
    g_qr2 = row(jnp.tile(q_rope_norm, (1, LANE // QK_ROPE)))
    g_kr2 = row(jnp.tile(k_rope_norm, (1, LANE // QK_ROPE)))
    g_kn_col = k_nope_norm[:, :, None]

    tm_p = _tile(tp, 512)
    tm_s = bs * ts
    cos_p, sin_p = _rope_tables(jnp.arange(tp, dtype=jnp.int32), 1)
    cos_s, sin_s = _rope_tables(past + jnp.arange(ts, dtype=jnp.int32), bs)

    tk_s = -(-(past + ts) // 512) * 512
    pad_s = tk_s - past - ts
    cache_krt = jnp.swapaxes(cache_k_rope, -1, -2).astype(BF16)

    mem_k_p, mem_v_p = _mem_kv(mem_prompt.reshape(bp * n_mem, d), row(mem_norm), w_ck_b, w_cv_b, row(ck_norm))
    mem_k_p = mem_k_p.reshape(n_layers, bp, n_mem, dm)
    mem_v_p = mem_v_p.reshape(n_layers, bp, n_mem, dm)
    mem_k_s = cache_mem_k.reshape(n_layers, bs, n_mem, dm)
    mem_v_s = cache_mem_v.reshape(n_layers, bs, n_mem, dm)
    zero_conv = jnp.zeros((bp, CONV_WIDTH - 1, d), F32)

    yp = x_prompt.reshape(bp * tp, d)
    ys = x_sample.reshape(bs * ts, d)
    lat_p, kr_p, cv_p, lat_s, kr_s, cv_s = [], [], [], [], [], []
    for l in range(n_layers):
        def mla(x, cos_t, sin_t, tm):
            return _mla_proj(x, row(mix_norm), w_small, row(q_a_norm), w_uq_p, row(q_nope_norm), g_qr2,
                             row(kv_a_norm), g_kr2, cos_t, sin_t, l, tm)

        yp = _ffn(yp, row(ffn1_norm), w1g, w1u, w1d, l)
        q, nl, nk, nkt = mla(yp, cos_p, sin_p, tm_p)
        kt, v = _kv_proj(nl.reshape(bp, tp, KV_LORA),
                         jnp.transpose(nkt.reshape(QK_ROPE, bp, tp), (1, 0, 2)), w_uk_t, w_uv, g_kn_col, l)
        attn = _attention(q, kt, v, t=tp, past=0, n_valid=tp)
        yp, nc = _mix(yp, row(mix_norm), attn, w_b, w_c, w_x, w_ga, w_gc, conv_w, zero_conv, w_o_b, l, tp, tm_p)
        yp = _cross(yp, row(cross_norm), w_cq_b, row(cq_norm), mem_k_p, mem_v_p, w_co_b, l, tp, tm_p, l)
        yp = _ffn(yp, row(ffn2_norm), w2g, w2u, w2d, l)
        lat_p.append(nl.reshape(bp, tp, KV_LORA))
        kr_p.append(nk.reshape(bp, tp, QK_ROPE))
        cv_p.append(nc)

        ys = _ffn(ys, row(ffn1_norm), w1g, w1u, w1d, l)
        q, nl, nk, nkt = mla(ys, cos_s, sin_s, tm_s)
        lat_all = jnp.concatenate([cache_kv_latent[l], nl.reshape(bs, ts, KV_LORA),
                                   jnp.zeros((bs, pad_s, KV_LORA), F32)], axis=1)
        krt_all = jnp.concatenate([cache_krt[l], jnp.transpose(nkt.reshape(QK_ROPE, bs, ts), (1, 0, 2)),
                                   jnp.zeros((bs, QK_ROPE, pad_s), BF16)], axis=2)
        kt, v = _kv_proj(lat_all, krt_all, w_uk_t, w_uv, g_kn_col, l)
        attn = _attention(q, kt, v, t=ts, past=past, n_valid=past + ts)
        ys, nc = _mix(ys, row(mix_norm), attn, w_b, w_c, w_x, w_ga, w_gc, conv_w, state_conv[l], w_o_b, l, ts, tm_s)
        ys = _cross(ys, row(cross_norm), w_cq_b, row(cq_norm), mem_k_s, mem_v_s, w_co_b, l, ts, tm_s, l)
        ys = _ffn(ys, row(ffn2_norm), w2g, w2u, w2d, l)
        lat_s.append(nl.reshape(bs, ts, KV_LORA))
        kr_s.append(nk.reshape(bs, ts, QK_ROPE))
        cv_s.append(nc)

    mem_shape = (n_layers, bp, n_mem, MEM_HEADS, MEM_HEAD_DIM)
    return (yp.reshape(bp, tp, d), ys.reshape(bs, ts, d), jnp.stack(lat_p), jnp.stack(kr_p), jnp.stack(cv_p),
            mem_k_p.reshape(mem_shape), mem_v_p.reshape(mem_shape), jnp.stack(lat_s), jnp.stack(kr_s),
            jnp.stack(cv_s))
```

```python
import functools
import math

import jax
import jax.numpy as jnp
from jax import lax
from jax.experimental import pallas as pl
from jax.experimental.pallas import tpu as pltpu

F32 = jnp.float32
BF16 = jnp.bfloat16

V_HEAD = 128
QK_NOPE = 128
QK_ROPE = 64
ROPE_HALF = QK_ROPE // 2
QK_DIM = QK_NOPE + QK_ROPE
Q_LORA = 768
KV_LORA = 512
CHUNK = 64
ROPE_BASE = 10000.0
MEM_HEADS = 4
MEM_HEAD_DIM = 128
CONV_WIDTH = 3
EPS = 1e-6
MLA_SCALE = 1.0 / math.sqrt(QK_NOPE + QK_ROPE)
MEM_SCALE = 1.0 / math.sqrt(MEM_HEAD_DIM)
Q_SCALE = MLA_SCALE * math.log2(math.e)

LANE = 128
V7X_VMEM_BYTES = 64 * 1024 * 1024
VMEM_LIMIT_BYTES = V7X_VMEM_BYTES * 13 // 16

SMALL_W = Q_LORA + KV_LORA + LANE


def _params(*sem):
    return pltpu.CompilerParams(dimension_semantics=sem, vmem_limit_bytes=VMEM_LIMIT_BYTES)


def _tile(n, target):
    t = min(n, target)
    while n % t:
        t -= 1
    return t


def _rms(x, g):
    ms = jnp.mean(x * x, axis=-1, keepdims=True)
    return x * lax.rsqrt(ms + EPS) * g


def _sigmoid(x):
    return 1.0 / (1.0 + jnp.exp(-x))


def _dot(a, b):
    return jnp.dot(a, b, preferred_element_type=F32)


def _dot_nt(a, b):
    return lax.dot_general(a, b, (((1,), (1,)), ((), ())), preferred_element_type=F32)


def _ffn_body(x_ref, g_ref, wg_ref, wu_ref, wd_ref, o_ref, h_ref, *, row_chunk):
    @pl.when(pl.program_id(1) == 0)
    def _():
        x = x_ref[...]
        h_ref[...] = _rms(x, g_ref[...]).astype(BF16)
        o_ref[...] = x

    tm = h_ref.shape[0]
    cm = _tile(tm, row_chunk)
    for r0 in range(0, tm, cm):
        h = h_ref[r0:r0 + cm, :]
        gate = _dot(h, wg_ref[...])
        up = _dot(h, wu_ref[...])
        act = (gate * _sigmoid(gate) * up * 0.5).astype(BF16)
        o_ref[r0:r0 + cm, :] += _dot(act, wd_ref[...])


def _ffn(x, g, wg, wu, wd, layer):
    n, d = x.shape
    f = wg.shape[-1]
    tm, tf = _tile(n, 512), _tile(f, 512)
    return pl.pallas_call(
        functools.partial(_ffn_body, row_chunk=tm),
        grid=(n // tm, f // tf),
        in_specs=[
            pl.BlockSpec((tm, d), lambda i, j: (i, 0)),
            pl.BlockSpec((None, 1, d), lambda i, j: (layer, 0, 0)),
            pl.BlockSpec((None, d, tf), lambda i, j: (layer, 0, j)),
            pl.BlockSpec((None, d, tf), lambda i, j: (layer, 0, j)),
            pl.BlockSpec((None, tf, d), lambda i, j: (layer, j, 0)),
        ],
        out_specs=pl.BlockSpec((tm, d), lambda i, j: (i, 0)),
        out_shape=jax.ShapeDtypeStruct((n, d), F32),
        scratch_shapes=[pltpu.VMEM((tm, d), BF16)],
        compiler_params=_params("parallel", "arbitrary"),
        name="ffn",
    )(x, g, wg, wu, wd)


def _mla_proj_body(x_ref, gmix_ref, ws_ref, gqa_ref, wuq_ref, gqn_ref, gqr_ref, gkva_ref, gkr_ref,
                   cos_ref, sin_ref, q_ref, lat_ref, kr_ref, krt_ref, *, n_heads, row_chunk):
    tm = x_ref.shape[0]
    cm = _tile(tm, row_chunk)
    lane = lax.broadcasted_iota(jnp.int32, (cm, LANE), 1)
    first_half = (lane & ROPE_HALF) == 0
    low_group = lane < QK_ROPE
    gqn = gqn_ref[...]
    gqr = gqr_ref[...]
    rope_base = n_heads * QK_NOPE

    for r0 in range(0, tm, cm):
        rs = slice(r0, r0 + cm)
        h = _rms(x_ref[rs, :], gmix_ref[...]).astype(BF16)
        u = _dot(h, ws_ref[...])
        c_q = u[:, :Q_LORA]
        c_kv = u[:, Q_LORA:Q_LORA + KV_LORA]
        k_r = u[:, Q_LORA + KV_LORA:]

        lat_ref[rs, :] = _rms(c_kv, gkva_ref[...])

        cos_t = cos_ref[rs, :]
        sin_t = sin_ref[rs, :]

        def rope(y):
            swapped = jnp.where(first_half, pltpu.roll(y, LANE - ROPE_HALF, 1), pltpu.roll(y, ROPE_HALF, 1))
            return y * cos_t + swapped * sin_t

        ms = jnp.sum(k_r * k_r, axis=-1, keepdims=True) * (1.0 / QK_ROPE)
        kr = rope(k_r * lax.rsqrt(ms + EPS) * gkr_ref[...])
        kr_ref[rs, :] = kr[:, :QK_ROPE]
        krt_ref[:, rs] = kr.T[:QK_ROPE, :].astype(BF16)

        cq_n = _rms(c_q, gqa_ref[...]).astype(BF16)
        q = _dot(cq_n, wuq_ref[...])
        for hd in range(n_heads):
            qn = _rms(q[:, hd * QK_NOPE:(hd + 1) * QK_NOPE], gqn) * Q_SCALE
            q_ref[hd, rs, :QK_NOPE] = qn.astype(BF16)
        for pair in range(n_heads // 2):
            col = q[:, rope_base + pair * LANE: rope_base + (pair + 1) * LANE]
            sq = col * col
            lo = jnp.sum(jnp.where(low_group, sq, 0.0), axis=-1, keepdims=True)
            hi = jnp.sum(jnp.where(low_group, 0.0, sq), axis=-1, keepdims=True)
            ms = jnp.where(low_group, lo, hi) * (1.0 / QK_ROPE)
            y = (rope(col * lax.rsqrt(ms + EPS) * gqr) * Q_SCALE).astype(BF16)
            q_ref[2 * pair, rs, QK_NOPE:] = y[:, :QK_ROPE]
            q_ref[2 * pair + 1, rs, QK_NOPE:] = y[:, QK_ROPE:]


def _mla_proj(x, gmix, ws, gqa, wuq, gqn, gqr2, gkva, gkr2, cos_t, sin_t, layer, tm):
    n, d = x.shape
    n_heads = wuq.shape[-1] // QK_DIM
    n_pos_tiles = cos_t.shape[0] // tm
    const = dict(pipeline_mode=pl.Buffered(1))
    return pl.pallas_call(
        functools.partial(_mla_proj_body, n_heads=n_heads, row_chunk=256),
        grid=(n // tm,),
        in_specs=[
            pl.BlockSpec((tm, d), lambda i: (i, 0)),
            pl.BlockSpec((None, 1, d), lambda i: (layer, 0, 0)),
            pl.BlockSpec((None, d, SMALL_W), lambda i: (layer, 0, 0), **const),
            pl.BlockSpec((None, 1, Q_LORA), lambda i: (layer, 0, 0)),
            pl.BlockSpec((None, Q_LORA, n_heads * QK_DIM), lambda i: (layer, 0, 0), **const),
            pl.BlockSpec((None, 1, QK_NOPE), lambda i: (layer, 0, 0)),
            pl.BlockSpec((None, 1, LANE), lambda i: (layer, 0, 0)),
            pl.BlockSpec((None, 1, KV_LORA), lambda i: (layer, 0, 0)),
            pl.BlockSpec((None, 1, LANE), lambda i: (layer, 0, 0)),
            pl.BlockSpec((tm, LANE), lambda i: (i % n_pos_tiles, 0)),
            pl.BlockSpec((tm, LANE), lambda i: (i % n_pos_tiles, 0)),
        ],
        out_specs=[
            pl.BlockSpec((n_heads, tm, QK_DIM), lambda i: (0, i, 0)),
            pl.BlockSpec((tm, KV_LORA), lambda i: (i, 0)),
            pl.BlockSpec((tm, QK_ROPE), lambda i: (i, 0)),
            pl.BlockSpec((QK_ROPE, tm), lambda i: (0, i)),
        ],
        out_shape=[
            jax.ShapeDtypeStruct((n_heads, n, QK_DIM), BF16),
            jax.ShapeDtypeStruct((n, KV_LORA), F32),
            jax.ShapeDtypeStruct((n, QK_ROPE), F32),
            jax.ShapeDtypeStruct((QK_ROPE, n), BF16),
        ],
        compiler_params=_params("parallel"),
        name="mla_proj",
    )(x, gmix, ws, gqa, wuq, gqn, gqr2, gkva, gkr2, cos_t, sin_t)


def _kv_proj_body(lat_ref, krt_ref, wukt_ref, wuv_ref, gkn_ref, kt_ref, v_ref, *, n_heads):
    lat = lat_ref[...].astype(BF16)
    knt = _dot_nt(wukt_ref[...], lat)
    krt = krt_ref[...]
    gkn = gkn_ref[...]
    for hd in range(n_heads):
        blk = knt[hd * QK_NOPE:(hd + 1) * QK_NOPE, :]
        ms = jnp.mean(blk * blk, axis=0, keepdims=True)
        kt_ref[hd, :QK_NOPE, :] = (blk * lax.rsqrt(ms + EPS) * gkn).astype(BF16)
        kt_ref[hd, QK_NOPE:, :] = krt
    v = _dot(lat, wuv_ref[...])
    for hd in range(n_heads):
        v_ref[hd] = v[:, hd * V_HEAD:(hd + 1) * V_HEAD].astype(BF16)


def _kv_proj(lat, krt, wukt, wuv, gkn_col, layer):
    b, tk, _ = lat.shape
    n_heads = wuv.shape[-1] // V_HEAD
    tm = _tile(tk, 512)
    const = dict(pipeline_mode=pl.Buffered(1))
    return pl.pallas_call(
        functools.partial(_kv_proj_body, n_heads=n_heads),
        grid=(b, tk // tm),
        in_specs=[
            pl.BlockSpec((None, tm, KV_LORA), lambda i, j: (i, j, 0)),
            pl.BlockSpec((None, QK_ROPE, tm), lambda i, j: (i, 0, j)),
            pl.BlockSpec((None, n_heads * QK_NOPE, KV_LORA), lambda i, j: (layer, 0, 0), **const),
            pl.BlockSpec((None, KV_LORA, n_heads * V_HEAD), lambda i, j: (layer, 0, 0), **const),
            pl.BlockSpec((None, QK_NOPE, 1), lambda i, j: (layer, 0, 0)),
        ],
        out_specs=[
            pl.BlockSpec((None, n_heads, QK_DIM, tm), lambda i, j: (i, 0, 0, j)),
            pl.BlockSpec((None, n_heads, tm, V_HEAD), lambda i, j: (i, 0, j, 0)),
        ],
        out_shape=[
            jax.ShapeDtypeStruct((b, n_heads, QK_DIM, tk), BF16),
            jax.ShapeDtypeStruct((b, n_heads, tk, V_HEAD), BF16),
        ],
        compiler_params=_params("parallel", "parallel"),
        name="kv_proj",
    )(lat, krt, wukt, wuv, gkn_col)


def _attn_body(q_ref, kt_ref, v_ref, o_ref, *, t, tq, past, n_valid, key_tile):
    tk = kt_ref.shape[-1]
    blocks = []
    for qs in range(0, t, tq):
        c_lo = (past + qs) // CHUNK
        c_hi = (past + qs + tq - 1) // CHUNK
        full_end = min((c_lo + 1) * CHUNK, n_valid) // LANE * LANE
        edge_end = min(-(-min((c_hi + 1) * CHUNK, n_valid) // LANE) * LANE, tk)
        tiles = [(lo, min(lo + key_tile, full_end), False) for lo in range(0, full_end, key_tile)]
        tiles += [(lo, min(lo + key_tile, edge_end), True) for lo in range(full_end, edge_end, key_tile)]
        blocks.append((qs, tiles))

    def scores(qs, tile):
        lo, hi, needs_mask = tile
        s = _dot(q_ref[qs:qs + tq, :], kt_ref[:, lo:hi])
        if needs_mask:
            q_chunk = (lax.broadcasted_iota(jnp.int32, (tq, hi - lo), 0) + (past + qs)) >> 6
            k_pos = lax.broadcasted_iota(jnp.int32, (tq, hi - lo), 1) + lo
            s = jnp.where(jnp.logical_and((k_pos >> 6) <= q_chunk, k_pos < n_valid), s, -jnp.inf)
        return s

    def lane_groups(a):
        return [a[:, c:c + LANE] for c in range(0, a.shape[1], LANE)]

    cur = [scores(blocks[0][0], tile) for tile in blocks[0][1]]
    for bi, (qs, tiles) in enumerate(blocks):
        nxt_qs, nxt_tiles = blocks[bi + 1] if bi + 1 < len(blocks) else (None, [])
        m = jnp.max(functools.reduce(jnp.maximum, [g for s in cur for g in lane_groups(s)]),
                    axis=-1, keepdims=True)
        nxt = []
        l_acc = 0.0
        ps = []
        for k, s in enumerate(cur):
            take = len(nxt_tiles) - len(nxt) if k == len(cur) - 1 else 1
            for tile in nxt_tiles[len(nxt):len(nxt) + take]:
                nxt.append(scores(nxt_qs, tile))
            p = jnp.exp2(s - m)
            l_acc = l_acc + functools.reduce(jnp.add, lane_groups(p))
            ps.append(p.astype(BF16))
        p_all = ps[0] if len(ps) == 1 else jnp.concatenate(ps, axis=1)
        o = _dot(p_all, v_ref[tiles[0][0]:tiles[-1][1], :])
        o_ref[qs:qs + tq, :] = o * (1.0 / jnp.sum(l_acc, axis=-1, keepdims=True))
        cur = nxt


def _attention(q, kt, v, *, t, past, n_valid):
    assert CHUNK == 64
    n_heads, n, _ = q.shape
    b, _, _, tk = kt.shape
    tq = _tile(t, 256)
    return pl.pallas_call(
        functools.partial(_attn_body, t=t, tq=tq, past=past, n_valid=n_valid, key_tile=256),
        grid=(b, n_heads),
        in_specs=[
            pl.BlockSpec((None, t, QK_DIM), lambda i, j: (j, i, 0)),
            pl.BlockSpec((None, None, QK_DIM, tk), lambda i, j: (i, j, 0, 0)),
            pl.BlockSpec((None, None, tk, V_HEAD), lambda i, j: (i, j, 0, 0)),
        ],
        out_specs=pl.BlockSpec((t, V_HEAD), lambda i, j: (i, j)),
        out_shape=jax.ShapeDtypeStruct((n, n_heads * V_HEAD), F32),
        compiler_params=_params("parallel", "parallel"),
        name="attention",
    )(q, kt, v)


def _mix_body(x_ref, g_ref, a_ref, wb_ref, wc_ref, wx_ref, wga_ref, wgc_ref, cw_ref, prev_ref, wo_ref,
              o_ref, nc_ref, h_ref, carry_ref, *, t_seq, tiles_per_seq, row_chunk):
    tm, tn = a_ref.shape
    i = pl.program_id(0)
    j = pl.program_id(1)

    @pl.when(j == 0)
    def _():
        x = x_ref[...]
        h_ref[...] = _rms(x, g_ref[...]).astype(BF16)
        o_ref[...] = x

    cw = cw_ref[...]
    if tiles_per_seq == 1:
        befores = [(s * t_seq, prev_ref[s, 0:1, :], prev_ref[s, 1:2, :]) for s in range(tm // t_seq)]
        cm = tm
    else:
        @pl.when(i % tiles_per_seq == 0)
        def _():
            carry_ref[j] = prev_ref[0]

        befores = [(0, carry_ref[j, 0:1, :], carry_ref[j, 1:2, :])]
        cm = _tile(tm, row_chunk)

    for r0 in range(0, tm, cm):
        h = h_ref[r0:r0 + cm, :]
        z = _dot(h, wc_ref[...]) * _dot(h, wx_ref[...])
        rows = lax.broadcasted_iota(jnp.int32, (cm, tn), 0)
        z1 = pltpu.roll(z, 1, 0)
        z2 = pltpu.roll(z, 2, 0)
        for row, p0, p1 in befores:
            z1 = jnp.where(rows == row, p1, z1)
            z2 = jnp.where(rows == row, p0, jnp.where(rows == row + 1, p1, z2))
        befores = [(0, z[cm - 2:cm - 1, :], z[cm - 1:, :])]
        y = cw[0:1, :] * z2 + cw[1:2, :] * z1 + cw[2:3, :] * z
        conv_out = _dot(h, wb_ref[...]) * y
        mixed = (_sigmoid(_dot(h, wga_ref[...])) * a_ref[r0:r0 + cm, :]
                 + _sigmoid(_dot(h, wgc_ref[...])) * conv_out)
        o_ref[r0:r0 + cm, :] += _dot(mixed.astype(BF16), wo_ref[...])
        if tiles_per_seq == 1:
            for s in range(tm // t_seq):
                nc_ref[s] = z[(s + 1) * t_seq - 2:(s + 1) * t_seq, :]
        elif r0 + cm == tm:
            carry_ref[j] = z[cm - 2:, :]
            nc_ref[0] = z[cm - 2:, :]


def _mix(x, g, attn, wb, wc, wx, wga, wgc, cw, prev, wo, layer, t_seq, tm):
    n, d = x.shape
    tn = _tile(d, 256)
    assert CONV_WIDTH == 3 and t_seq >= CONV_WIDTH - 1
    if tm >= t_seq:
        assert tm % t_seq == 0
        n_seq, tiles_per_seq = tm // t_seq, 1
        prev_map = lambda i, j: (i, 0, j)
    else:
        assert t_seq % tm == 0
        n_seq, tiles_per_seq = 1, t_seq // tm
        prev_map = lambda i, j: (i // tiles_per_seq, 0, j)
    w_spec = pl.BlockSpec((None, d, tn), lambda i, j: (layer, 0, j))
    y, tail_rows = pl.pallas_call(
        functools.partial(_mix_body, t_seq=t_seq, tiles_per_seq=tiles_per_seq, row_chunk=128),
        grid=(n // tm, d // tn),
        in_specs=[
            pl.BlockSpec((tm, d), lambda i, j: (i, 0)),
            pl.BlockSpec((None, 1, d), lambda i, j: (layer, 0, 0)),
            pl.BlockSpec((tm, tn), lambda i, j: (i, j)),
            w_spec, w_spec, w_spec, w_spec, w_spec,
            pl.BlockSpec((None, CONV_WIDTH, tn), lambda i, j: (layer, 0, j)),
            pl.BlockSpec((n_seq, CONV_WIDTH - 1, tn), prev_map),
            pl.BlockSpec((None, tn, d), lambda i, j: (layer, j, 0)),
        ],
        out_specs=[
            pl.BlockSpec((tm, d), lambda i, j: (i, 0)),
            pl.BlockSpec((n_seq, CONV_WIDTH - 1, tn), lambda i, j: (i, 0, j)),
        ],
        out_shape=[
            jax.ShapeDtypeStruct((n, d), F32),
            jax.ShapeDtypeStruct((n // tm * n_seq, CONV_WIDTH - 1, d), F32),
        ],
        scratch_shapes=[pltpu.VMEM((tm, d), BF16), pltpu.VMEM((d // tn, CONV_WIDTH - 1, tn), F32)],
        compiler_params=_params("arbitrary", "arbitrary"),
        name="mix",
    )(x, g, attn, wb, wc, wx, wga, wgc, cw, prev, wo)
    return y, tail_rows.reshape(-1, tiles_per_seq, CONV_WIDTH - 1, d)[:, -1]


def _cross_body(x_ref, g_ref, wq_ref, gq_ref, mk_ref, mv_ref, wo_ref, o_ref, att_ref, *, t_seq):
    tm = x_ref.shape[0]
    x = x_ref[...]
    h = _rms(x, g_ref[...]).astype(BF16)
    q = _dot(h, wq_ref[...])
    gq = gq_ref[...]
    rows = min(t_seq, tm)
    for s in range(tm // rows):
        r0 = s * rows
        for hd in range(MEM_HEADS):
            c0 = hd * MEM_HEAD_DIM
            qh = (_rms(q[r0:r0 + rows, c0:c0 + MEM_HEAD_DIM], gq) * MEM_SCALE).astype(BF16)
            kh = mk_ref[s, :, c0:c0 + MEM_HEAD_DIM].astype(BF16)
            vh = mv_ref[s, :, c0:c0 + MEM_HEAD_DIM].astype(BF16)
            sc = _dot_nt(qh, kh)
            p = jnp.exp(sc - jnp.max(sc, axis=-1, keepdims=True))
            l = jnp.sum(p, axis=-1, keepdims=True)
            att_ref[r0:r0 + rows, c0:c0 + MEM_HEAD_DIM] = (_dot(p.astype(BF16), vh) * (1.0 / l)).astype(BF16)
    o_ref[...] = x + _dot(att_ref[...], wo_ref[...])


def _cross(x, g, wq, gq, mem_k, mem_v, wo, layer, t_seq, tm, mem_layer):
    n, d = x.shape
    _, _, n_mem, dm = mem_k.shape
    if tm >= t_seq:
        n_seq = tm // t_seq
        mem_map = lambda i: (mem_layer, i, 0, 0)
    else:
        n_seq, tiles_per_seq = 1, t_seq // tm
        mem_map = lambda i: (mem_layer, i // tiles_per_seq, 0, 0)
    const = dict(pipeline_mode=pl.Buffered(1))
    return pl.pallas_call(
        functools.partial(_cross_body, t_seq=t_seq),
        grid=(n // tm,),
        in_specs=[
            pl.BlockSpec((tm, d), lambda i: (i, 0)),
            pl.BlockSpec((None, 1, d), lambda i: (layer, 0, 0)),
            pl.BlockSpec((None, d, dm), lambda i: (layer, 0, 0), **const),
            pl.BlockSpec((None, 1, MEM_HEAD_DIM), lambda i: (layer, 0, 0)),
            pl.BlockSpec((None, n_seq, n_mem, dm), mem_map),
            pl.BlockSpec((None, n_seq, n_mem, dm), mem_map),
            pl.BlockSpec((None, dm, d), lambda i: (layer, 0, 0), **const),
        ],
        out_specs=pl.BlockSpec((tm, d), lambda i: (i, 0)),
        out_shape=jax.ShapeDtypeStruct((n, d), F32),
        scratch_shapes=[pltpu.VMEM((tm, dm), BF16)],
        compiler_params=_params("parallel"),
        name="cross",
    )(x, g, wq, gq, mem_k, mem_v, wo)


def _mem_kv_body(m_ref, g_ref, wk_ref, wv_ref, gk_ref, k_ref, v_ref):
    m = _rms(m_ref[...], g_ref[...]).astype(BF16)
    k = _dot(m, wk_ref[...])
    gk = gk_ref[...]
    for hd in range(MEM_HEADS):
        c0 = hd * MEM_HEAD_DIM
        k_ref[:, c0:c0 + MEM_HEAD_DIM] = _rms(k[:, c0:c0 + MEM_HEAD_DIM], gk)
    v_ref[...] = _dot(m, wv_ref[...])


def _mem_kv(mem, g, wk, wv, gk):
    n, d = mem.shape
    n_layers, _, dm = wk.shape
    tm = _tile(n, 512)
    out = jax.ShapeDtypeStruct((n_layers, n, dm), F32)
    return pl.pallas_call(
        _mem_kv_body,
        grid=(n_layers, n // tm),
        in_specs=[
            pl.BlockSpec((tm, d), lambda l, i: (i, 0)),
            pl.BlockSpec((None, 1, d), lambda l, i: (l, 0, 0)),
            pl.BlockSpec((None, d, dm), lambda l, i: (l, 0, 0)),
            pl.BlockSpec((None, d, dm), lambda l, i: (l, 0, 0)),
            pl.BlockSpec((None, 1, MEM_HEAD_DIM), lambda l, i: (l, 0, 0)),
        ],
        out_specs=[
            pl.BlockSpec((None, tm, dm), lambda l, i: (l, i, 0)),
            pl.BlockSpec((None, tm, dm), lambda l, i: (l, i, 0)),
        ],
        out_shape=[out, out],
        compiler_params=_params("parallel", "parallel"),
        name="mem_kv",
    )(mem, g, wk, wv, gk)


def _rope_tables(pos, reps):
    inv = ROPE_BASE ** (-jnp.arange(ROPE_HALF, dtype=F32) / ROPE_HALF)
    ang = pos.astype(F32)[:, None] * inv[None, :]
    cos, sin = jnp.cos(ang), jnp.sin(ang)
    groups = LANE // QK_ROPE
    cos_t = jnp.tile(jnp.concatenate([cos, cos], axis=1), (reps, groups))
    sin_t = jnp.tile(jnp.concatenate([-sin, sin], axis=1), (reps, groups))
    return cos_t, sin_t


def kernel(x_prompt, x_sample, mem_prompt, cache_kv_latent, cache_k_rope, state_conv, cache_mem_k, cache_mem_v, ffn1_norm, ffn1_w_gate, ffn1_w_up, ffn1_w_down, mix_norm, w_in, q_a_norm, w_uq, q_nope_norm, q_rope_norm, kv_a_norm, w_ukv, k_nope_norm, k_rope_norm, conv_w, w_o, cross_norm, mem_norm, w_cq, w_ck, w_cv, cq_norm, ck_norm, w_co, ffn2_norm, ffn2_w_gate, ffn2_w_up, ffn2_w_down):
    bp, tp, d = x_prompt.shape
    bs, ts, _ = x_sample.shape
    n_layers = w_in.shape[0]
    past = cache_kv_latent.shape[2]
    n_mem = mem_prompt.shape[1]
    n_heads = w_uq.shape[-1] // QK_DIM
    dm = MEM_HEADS * MEM_HEAD_DIM

    row = lambda a: a[:, None, :]
    w1g, w1u, w1d = ffn1_w_gate.astype(BF16), ffn1_w_up.astype(BF16), ffn1_w_down.astype(BF16)
    w2g, w2u, w2d = ffn2_w_gate.astype(BF16), ffn2_w_up.astype(BF16), ffn2_w_down.astype(BF16)
    n_small = Q_LORA + KV_LORA + QK_ROPE
    w_small = jnp.pad(w_in[:, :, :n_small], ((0, 0), (0, 0), (0, SMALL_W - n_small))).astype(BF16)
    w_b, w_c, w_x, w_ga, w_gc = [w_in[:, :, n_small + k * d: n_small + (k + 1) * d].astype(BF16) for k in range(5)]
    uq = w_uq.reshape(n_layers, Q_LORA, n_heads, QK_DIM)
    w_uq_p = jnp.concatenate([uq[..., :QK_NOPE].reshape(n_layers, Q_LORA, n_heads * QK_NOPE),
                              uq[..., QK_NOPE:].reshape(n_layers, Q_LORA, n_heads * QK_ROPE)], axis=-1).astype(BF16)
    ukv = w_ukv.reshape(n_layers, KV_LORA, n_heads, QK_NOPE + V_HEAD)
    w_uk_t = jnp.transpose(ukv[..., :QK_NOPE], (0, 2, 3, 1)).reshape(n_layers, n_heads * QK_NOPE, KV_LORA).astype(BF16)
    w_uv = ukv[..., QK_NOPE:].reshape(n_layers, KV_LORA, n_heads * V_HEAD).astype(BF16)
    w_o_b = w_o.astype(BF16)
    w_cq_b, w_ck_b, w_cv_b, w_co_b = w_cq.astype(BF16), w_ck.astype(BF16), w_cv.astype(BF16), w_co.astype(BF16)
    g_qr2 = row(jnp.tile(q_rope_norm, (1, LANE // QK_ROPE)))
    g_kr2 = row(jnp.tile(k_rope_norm, (1, LANE // QK_ROPE)))
    g_kn_col = k_nope_norm[:, :, None]

    tm_p = _tile(tp, 512)
    tm_s = bs * ts
    cos_p, sin_p = _rope_tables(jnp.arange(tp, dtype=jnp.int32), 1)
    cos_s, sin_s = _rope_tables(past + jnp.arange(ts, dtype=jnp.int32), bs)

    tk_s = -(-(past + ts) // 512) * 512
    pad_s = tk_s - past - ts
    cache_krt = jnp.swapaxes(cache_k_rope, -1, -2).astype(BF16)

    mem_k_p, mem_v_p = _mem_kv(mem_prompt.reshape(bp * n_mem, d), row(mem_norm), w_ck_b, w_cv_b, row(ck_norm))
    mem_k_p = mem_k_p.reshape(n_layers, bp, n_mem, dm)
    mem_v_p = mem_v_p.reshape(n_layers, bp, n_mem, dm)
    mem_k_s = cache_mem_k.reshape(n_layers, bs, n_mem, dm)
    mem_v_s = cache_mem_v.reshape(n_layers, bs, n_mem, dm)
    zero_conv = jnp.zeros((bp, CONV_WIDTH - 1, d), F32)

    yp = x_prompt.reshape(bp * tp, d)
    ys = x_sample.reshape(bs * ts, d)
    lat_p, kr_p, cv_p, lat_s, kr_s, cv_s = [], [], [], [], [], []
    for l in range(n_layers):
        def mla(x, cos_t, sin_t, tm):
            return _mla_proj(x, row(mix_norm), w_small, row(q_a_norm), w_uq_p, row(q_nope_norm), g_qr2,
                             row(kv_a_norm), g_kr2, cos_t, sin_t, l, tm)

        yp = _ffn(yp, row(ffn1_norm), w1g, w1u, w1d, l)
        q, nl, nk, nkt = mla(yp, cos_p, sin_p, tm_p)
        kt, v = _kv_proj(nl.reshape(bp, tp, KV_LORA),
                         jnp.transpose(nkt.reshape(QK_ROPE, bp, tp), (1, 0, 2)), w_uk_t, w_uv, g_kn_col, l)
        attn = _attention(q, kt, v, t=tp, past=0, n_valid=tp)
        yp, nc = _mix(yp, row(mix_norm), attn, w_b, w_c, w_x, w_ga, w_gc, conv_w, zero_conv, w_o_b, l, tp, tm_p)
        yp = _cross(yp, row(cross_norm), w_cq_b, row(cq_norm), mem_k_p, mem_v_p, w_co_b, l, tp, tm_p, l)
        yp = _ffn(yp, row(ffn2_norm), w2g, w2u, w2d, l)
        lat_p.append(nl.reshape(bp, tp, KV_LORA))
        kr_p.append(nk.reshape(bp, tp, QK_ROPE))
        cv_p.append(nc)

        ys = _ffn(ys, row(ffn1_norm), w1g, w1u, w1d, l)
        q, nl, nk, nkt = mla(ys, cos_s, sin_s, tm_s)
        lat_all = jnp.concatenate([cache_kv_latent[l], nl.reshape(bs, ts, KV_LORA),
                                   jnp.zeros((bs, pad_s, KV_LORA), F32)], axis=1)
        krt_all = jnp.concatenate([cache_krt[l], jnp.transpose(nkt.reshape(QK_ROPE, bs, ts), (1, 0, 2)),
                                   jnp.zeros((bs, QK_ROPE, pad_s), BF16)], axis=2)
        kt, v = _kv_proj(lat_all, krt_all, w_uk_t, w_uv, g_kn_col, l)
        attn = _attention(q, kt, v, t=ts, past=past, n_valid=past + ts)
        ys, nc = _mix(ys, row(mix_norm), attn, w_b, w_c, w_x, w_ga, w_gc, conv_w, state_conv[l], w_o_b, l, ts, tm_s)
        ys = _cross(ys, row(cross_norm), w_cq_b, row(cq_norm), mem_k_s, mem_v_s, w_co_b, l, ts, tm_s, l)
        ys = _ffn(ys, row(ffn2_norm), w2g, w2u, w2d, l)
        lat_s.append(nl.reshape(bs, ts, KV_LORA))
        kr_s.append(nk.reshape(bs, ts, QK_ROPE))
        cv_s.append(nc)

    mem_shape = (n_layers, bp, n_mem, MEM_HEADS, MEM_HEAD_DIM)
    return (yp.reshape(bp, tp, d), ys.reshape(bs, ts, d), jnp.stack(lat_p), jnp.stack(kr_p), jnp.stack(cv_p),
            mem_k_p.reshape(mem_shape), mem_v_p.reshape(mem_shape), jnp.stack(lat_s), jnp.stack(kr_s),
            jnp.stack(cv_s))
```

```python
import functools
import math

import jax
import jax.numpy as jnp
from jax import lax
from jax.experimental import pallas as pl
from jax.experimental.pallas import tpu as pltpu

F32 = jnp.float32
BF16 = jnp.bfloat16

V_HEAD = 128
QK_NOPE = 128
QK_ROPE = 64
ROPE_HALF = QK_ROPE // 2
QK_DIM = QK_NOPE + QK_ROPE
Q_LORA = 768
KV_LORA = 512
CHUNK = 64
ROPE_BASE = 10000.0
MEM_HEADS = 4
MEM_HEAD_DIM = 128
CONV_WIDTH = 3
EPS = 1e-6
MLA_SCALE = 1.0 / math.sqrt(QK_NOPE + QK_ROPE)
MEM_SCALE = 1.0 / math.sqrt(MEM_HEAD_DIM)
LOG2_E = math.log2(math.e)
Q_SCALE = MLA_SCALE * LOG2_E

LANE = 128
V7X_VMEM_BYTES = 64 * 1024 * 1024
VMEM_LIMIT_BYTES = V7X_VMEM_BYTES * 13 // 16

FFN_TILE = 512
MIX_TILE = 256
SMALL_W = Q_LORA + KV_LORA + LANE


def _params(*sem):
    return pltpu.CompilerParams(dimension_semantics=sem, vmem_limit_bytes=VMEM_LIMIT_BYTES)


def _tile(n, target):
    t = min(n, target)
    while n % t:
        t -= 1
    return t


def _rms(x, g):
    ms = jnp.mean(x * x, axis=-1, keepdims=True)
    return x * lax.rsqrt(ms + EPS) * g


def _sigmoid(x):
    return 1.0 / (1.0 + jnp.exp(-x))


def _dot(a, b):
    return jnp.dot(a, b, preferred_element_type=F32)


def _dot_nt(a, b):
    return lax.dot_general(a, b, (((1,), (1,)), ((), ())), preferred_element_type=F32)


def _ffn_body(x_ref, g_ref, wgu_ref, wd_ref, o_ref, h_ref):
    @pl.when(pl.program_id(1) == 0)
    def _():
        x = x_ref[...]
        h_ref[...] = _rms(x, g_ref[...]).astype(BF16)
        o_ref[...] = x

    tf = wd_ref.shape[0]
    gu = _dot(h_ref[...], wgu_ref[...])
    gate = gu[:, :tf]
    act = (gate * _sigmoid(gate) * gu[:, tf:] * 0.5).astype(BF16)
    o_ref[...] += _dot(act, wd_ref[...])


def _ffn_weights(wg, wu, wd):
    n_layers, d, f = wg.shape
    tf = _tile(f, FFN_TILE)
    parts = [w.astype(BF16).reshape(n_layers, d, f // tf, tf) for w in (wg, wu)]
    return jnp.concatenate(parts, axis=-1).reshape(n_layers, d, 2 * f), wd.astype(BF16)


def _ffn(x, g, wgu, wd, layer):
    n, d = x.shape
    f = wd.shape[1]
    tm, tf = _tile(n, 512), _tile(f, FFN_TILE)
    return pl.pallas_call(
        _ffn_body,
        grid=(n // tm, f // tf),
        in_specs=[
            pl.BlockSpec((tm, d), lambda i, j: (i, 0)),
            pl.BlockSpec((None, 1, d), lambda i, j: (layer, 0, 0)),
            pl.BlockSpec((None, d, 2 * tf), lambda i, j: (layer, 0, j)),
            pl.BlockSpec((None, tf, d), lambda i, j: (layer, j, 0)),
        ],
        out_specs=pl.BlockSpec((tm, d), lambda i, j: (i, 0)),
        out_shape=jax.ShapeDtypeStruct((n, d), F32),
        scratch_shapes=[pltpu.VMEM((tm, d), BF16)],
        compiler_params=_params("parallel", "arbitrary"),
        name="ffn",
    )(x, g, wgu, wd)


def _mla_proj_body(x_ref, gmix_ref, ws_ref, gqa_ref, wuq_ref, gqn_ref, gqr_ref, gkva_ref, gkr_ref,
                   cos_ref, sin_ref, q_ref, lat_ref, kr_ref, krt_ref, *, n_heads, row_chunk):
    tm = x_ref.shape[0]
    cm = _tile(tm, row_chunk)
    lane = lax.broadcasted_iota(jnp.int32, (cm, LANE), 1)
    first_half = (lane & ROPE_HALF) == 0
    low_group = lane < QK_ROPE
    gqn = gqn_ref[...]
    gqr = gqr_ref[...]
    rope_base = n_heads * QK_NOPE

    for r0 in range(0, tm, cm):
        rs = slice(r0, r0 + cm)
        h = _rms(x_ref[rs, :], gmix_ref[...]).astype(BF16)
        u = _dot(h, ws_ref[...])
        c_q = u[:, :Q_LORA]
        c_kv = u[:, Q_LORA:Q_LORA + KV_LORA]
        k_r = u[:, Q_LORA + KV_LORA:]

        lat_ref[rs, :] = _rms(c_kv, gkva_ref[...])

        cos_t = cos_ref[rs, :]
        sin_t = sin_ref[rs, :]

        def rope(y):
            swapped = jnp.where(first_half, pltpu.roll(y, LANE - ROPE_HALF, 1), pltpu.roll(y, ROPE_HALF, 1))
            return y * cos_t + swapped * sin_t

        ms = jnp.sum(k_r * k_r, axis=-1, keepdims=True) * (1.0 / QK_ROPE)
        kr = rope(k_r * lax.rsqrt(ms + EPS) * gkr_ref[...])
        kr_ref[rs, :] = kr[:, :QK_ROPE]
        krt_ref[:, rs] = kr.T[:QK_ROPE, :].astype(BF16)

        cq_n = _rms(c_q, gqa_ref[...]).astype(BF16)
        q = _dot(cq_n, wuq_ref[...])
        for hd in range(n_heads):
            qn = _rms(q[:, hd * QK_NOPE:(hd + 1) * QK_NOPE], gqn) * Q_SCALE
            q_ref[hd, rs, :QK_NOPE] = qn.astype(BF16)
        for pair in range(n_heads // 2):
            col = q[:, rope_base + pair * LANE: rope_base + (pair + 1) * LANE]
            sq = col * col
            lo = jnp.sum(jnp.where(low_group, sq, 0.0), axis=-1, keepdims=True)
            hi = jnp.sum(jnp.where(low_group, 0.0, sq), axis=-1, keepdims=True)
            ms = jnp.where(low_group, lo, hi) * (1.0 / QK_ROPE)
            y = (rope(col * lax.rsqrt(ms + EPS) * gqr) * Q_SCALE).astype(BF16)
            q_ref[2 * pair, rs, QK_NOPE:] = y[:, :QK_ROPE]
            q_ref[2 * pair + 1, rs, QK_NOPE:] = y[:, QK_ROPE:]


def _mla_proj(x, gmix, ws, gqa, wuq, gqn, gqr2, gkva, gkr2, cos_t, sin_t, layer, tm):
    n, d = x.shape
    n_heads = wuq.shape[-1] // QK_DIM
    n_pos_tiles = cos_t.shape[0] // tm
    const = dict(pipeline_mode=pl.Buffered(1))
    return pl.pallas_call(
        functools.partial(_mla_proj_body, n_heads=n_heads, row_chunk=256),
        grid=(n // tm,),
        in_specs=[
            pl.BlockSpec((tm, d), lambda i: (i, 0)),
            pl.BlockSpec((None, 1, d), lambda i: (layer, 0, 0)),
            pl.BlockSpec((None, d, SMALL_W), lambda i: (layer, 0, 0), **const),
            pl.BlockSpec((None, 1, Q_LORA), lambda i: (layer, 0, 0)),
            pl.BlockSpec((None, Q_LORA, n_heads * QK_DIM), lambda i: (layer, 0, 0), **const),
            pl.BlockSpec((None, 1, QK_NOPE), lambda i: (layer, 0, 0)),
            pl.BlockSpec((None, 1, LANE), lambda i: (layer, 0, 0)),
            pl.BlockSpec((None, 1, KV_LORA), lambda i: (layer, 0, 0)),
            pl.BlockSpec((None, 1, LANE), lambda i: (layer, 0, 0)),
            pl.BlockSpec((tm, LANE), lambda i: (i % n_pos_tiles, 0)),
            pl.BlockSpec((tm, LANE), lambda i: (i % n_pos_tiles, 0)),
        ],
        out_specs=[
            pl.BlockSpec((n_heads, tm, QK_DIM), lambda i: (0, i, 0)),
            pl.BlockSpec((tm, KV_LORA), lambda i: (i, 0)),
            pl.BlockSpec((tm, QK_ROPE), lambda i: (i, 0)),
            pl.BlockSpec((QK_ROPE, tm), lambda i: (0, i)),
        ],
        out_shape=[
            jax.ShapeDtypeStruct((n_heads, n, QK_DIM), BF16),
            jax.ShapeDtypeStruct((n, KV_LORA), F32),
            jax.ShapeDtypeStruct((n, QK_ROPE), F32),
            jax.ShapeDtypeStruct((QK_ROPE, n), BF16),
        ],
        compiler_params=_params("parallel"),
        name="mla_proj",
    )(x, gmix, ws, gqa, wuq, gqn, gqr2, gkva, gkr2, cos_t, sin_t)


def _kv_proj_body(lat_ref, krt_ref, wukt_ref, wuv_ref, gkn_ref, kt_ref, v_ref, *, n_heads):
    lat = lat_ref[...].astype(BF16)
    knt = _dot_nt(wukt_ref[...], lat)
    krt = krt_ref[...]
    gkn = gkn_ref[...]
    for hd in range(n_heads):
        blk = knt[hd * QK_NOPE:(hd + 1) * QK_NOPE, :]
        ms = jnp.mean(blk * blk, axis=0, keepdims=True)
        kt_ref[hd, :QK_NOPE, :] = (blk * lax.rsqrt(ms + EPS) * gkn).astype(BF16)
        kt_ref[hd, QK_NOPE:, :] = krt
    v = _dot(lat, wuv_ref[...])
    for hd in range(n_heads):
        v_ref[hd] = v[:, hd * V_HEAD:(hd + 1) * V_HEAD].astype(BF16)


def _kv_proj(lat, krt, wukt, wuv, gkn_col, layer):
    b, tk, _ = lat.shape
    n_heads = wuv.shape[-1] // V_HEAD
    tm = _tile(tk, 512)
    const = dict(pipeline_mode=pl.Buffered(1))
    return pl.pallas_call(
        functools.partial(_kv_proj_body, n_heads=n_heads),
        grid=(b, tk // tm),
        in_specs=[
            pl.BlockSpec((None, tm, KV_LORA), lambda i, j: (i, j, 0)),
            pl.BlockSpec((None, QK_ROPE, tm), lambda i, j: (i, 0, j)),
            pl.BlockSpec((None, n_heads * QK_NOPE, KV_LORA), lambda i, j: (layer, 0, 0), **const),
            pl.BlockSpec((None, KV_LORA, n_heads * V_HEAD), lambda i, j: (layer, 0, 0), **const),
            pl.BlockSpec((None, QK_NOPE, 1), lambda i, j: (layer, 0, 0)),
        ],
        out_specs=[
            pl.BlockSpec((None, n_heads, QK_DIM, tm), lambda i, j: (i, 0, 0, j)),
            pl.BlockSpec((None, n_heads, tm, V_HEAD), lambda i, j: (i, 0, j, 0)),
        ],
        out_shape=[
            jax.ShapeDtypeStruct((b, n_heads, QK_DIM, tk), BF16),
            jax.ShapeDtypeStruct((b, n_heads, tk, V_HEAD), BF16),
        ],
        compiler_params=_params("parallel", "parallel"),
        name="kv_proj",
    )(lat, krt, wukt, wuv, gkn_col)


def _attn_body(q_ref, kt_ref, v_ref, o_ref, *, t, tq, past, n_valid, key_tile):
    tk = kt_ref.shape[-1]
    blocks = []
    for qs in range(0, t, tq):
        c_lo = (past + qs) // CHUNK
        c_hi = (past + qs + tq - 1) // CHUNK
        full_end = min((c_lo + 1) * CHUNK, n_valid) // LANE * LANE
        edge_end = min(-(-min((c_hi + 1) * CHUNK, n_valid) // LANE) * LANE, tk)
        tiles = [(lo, min(lo + key_tile, full_end), False) for lo in range(0, full_end, key_tile)]
        tiles += [(lo, min(lo + key_tile, edge_end), True) for lo in range(full_end, edge_end, key_tile)]
        blocks.append((qs, tiles))

    def scores(qs, tile):
        lo, hi, needs_mask = tile
        s = _dot(q_ref[qs:qs + tq, :], kt_ref[:, lo:hi])
        if needs_mask:
            q_chunk = (lax.broadcasted_iota(jnp.int32, (tq, hi - lo), 0) + (past + qs)) >> 6
            k_pos = lax.broadcasted_iota(jnp.int32, (tq, hi - lo), 1) + lo
            s = jnp.where(jnp.logical_and((k_pos >> 6) <= q_chunk, k_pos < n_valid), s, -jnp.inf)
        return s

    def lane_groups(a):
        return [a[:, c:c + LANE] for c in range(0, a.shape[1], LANE)]

    cur = [scores(blocks[0][0], tile) for tile in blocks[0][1]]
    for bi, (qs, tiles) in enumerate(blocks):
        nxt_qs, nxt_tiles = blocks[bi + 1] if bi + 1 < len(blocks) else (None, [])
        m = jnp.max(functools.reduce(jnp.maximum, [g for s in cur for g in lane_groups(s)]),
                    axis=-1, keepdims=True)
        nxt = []
        l_acc = 0.0
        ps = []
        for k, s in enumerate(cur):
            take = len(nxt_tiles) - len(nxt) if k == len(cur) - 1 else 1
            for tile in nxt_tiles[len(nxt):len(nxt) + take]:
                nxt.append(scores(nxt_qs, tile))
            p = jnp.exp2(s - m)
            l_acc = l_acc + functools.reduce(jnp.add, lane_groups(p))
            ps.append(p.astype(BF16))
        p_all = ps[0] if len(ps) == 1 else jnp.concatenate(ps, axis=1)
        o = _dot(p_all, v_ref[tiles[0][0]:tiles[-1][1], :])
        o_ref[qs:qs + tq, :] = o * (1.0 / jnp.sum(l_acc, axis=-1, keepdims=True))
        cur = nxt


def _attention(q, kt, v, *, t, past, n_valid):
    assert CHUNK == 64
    n_heads, n, _ = q.shape
    b, _, _, tk = kt.shape
    tq = _tile(t, 256)
    heads_per_tile = MIX_TILE // V_HEAD
    return pl.pallas_call(
        functools.partial(_attn_body, t=t, tq=tq, past=past, n_valid=n_valid, key_tile=256),
        grid=(b, n_heads),
        in_specs=[
            pl.BlockSpec((None, t, QK_DIM), lambda i, j: (j, i, 0)),
            pl.BlockSpec((None, None, QK_DIM, tk), lambda i, j: (i, j, 0, 0)),
            pl.BlockSpec((None, None, tk, V_HEAD), lambda i, j: (i, j, 0, 0)),
        ],
        out_specs=pl.BlockSpec((None, t, V_HEAD), lambda i, j: (j // heads_per_tile, i, j % heads_per_tile)),
        out_shape=jax.ShapeDtypeStruct((n_heads // heads_per_tile, n, MIX_TILE), F32),
        compiler_params=_params("parallel", "parallel"),
        name="attention",
    )(q, kt, v)


def _mix_body(x_ref, g_ref, a_ref, w5_ref, cw_ref, prev_ref, wo_ref, o_ref, nc_ref, h_ref, carry_ref,
              *, t_seq, tiles_per_seq, row_chunk):
    _, tm, tn = a_ref.shape
    i = pl.program_id(0)
    j = pl.program_id(1)

    @pl.when(j == 0)
    def _():
        x = x_ref[...]
        h_ref[...] = _rms(x, g_ref[...]).astype(BF16)
        o_ref[...] = x

    cw = cw_ref[j]
    if tiles_per_seq == 1:
        befores = [(s * t_seq, prev_ref[s, j, 0:1, :], prev_ref[s, j, 1:2, :]) for s in range(tm // t_seq)]
        cm = tm
    else:
        @pl.when(i % tiles_per_seq == 0)
        def _():
            carry_ref[j] = prev_ref[0, j]

        befores = [(0, carry_ref[j, 0:1, :], carry_ref[j, 1:2, :])]
        cm = _tile(tm, row_chunk)

    for r0 in range(0, tm, cm):
        u = _dot(h_ref[r0:r0 + cm, :], w5_ref[...])
        u_b, u_c, u_x, g_a, g_c = [u[:, k * tn:(k + 1) * tn] for k in range(5)]
        z = u_c * u_x
        rows = lax.broadcasted_iota(jnp.int32, (cm, tn), 0)
        z1 = pltpu.roll(z, 1, 0)
        z2 = pltpu.roll(z, 2, 0)
        for row, p0, p1 in befores:
            z1 = jnp.where(rows == row, p1, z1)
            z2 = jnp.where(rows == row, p0, jnp.where(rows == row + 1, p1, z2))
        befores = [(0, z[cm - 2:cm - 1, :], z[cm - 1:, :])]
        y = cw[0:1, :] * z2 + cw[1:2, :] * z1 + cw[2:3, :] * z
        mixed = _sigmoid(g_a) * a_ref[j, r0:r0 + cm, :] + _sigmoid(g_c) * (u_b * y)
        o_ref[r0:r0 + cm, :] += _dot(mixed.astype(BF16), wo_ref[...])
        if tiles_per_seq == 1:
            for s in range(tm // t_seq):
                nc_ref[s, j] = z[(s + 1) * t_seq - 2:(s + 1) * t_seq, :]
        elif r0 + cm == tm:
            carry_ref[j] = z[cm - 2:, :]
            nc_ref[0, j] = z[cm - 2:, :]


def _col_tiled(a, tn):
    *lead, r, d = a.shape
    return jnp.swapaxes(a.reshape(*lead, r, d // tn, tn), -3, -2)


def _mix_weights(w_in, conv_w, w_o, n_small):
    n_layers, d, _ = w_in.shape
    tn = _tile(d, MIX_TILE)
    parts = [w_in[:, :, n_small + k * d: n_small + (k + 1) * d].astype(BF16).reshape(n_layers, d, d // tn, tn)
             for k in range(5)]
    w5 = jnp.concatenate(parts, axis=-1).reshape(n_layers, d, 5 * d)
    return w5, _col_tiled(conv_w, tn), w_o.astype(BF16)


def _mix(x, g, attn, w5, cw, prev, wo, layer, t_seq, tm):
    n, d = x.shape
    n_col, _, tn = attn.shape
    assert CONV_WIDTH == 3 and t_seq >= CONV_WIDTH - 1
    if tm >= t_seq:
        assert tm % t_seq == 0
        n_seq, tiles_per_seq = tm // t_seq, 1
    else:
        assert t_seq % tm == 0
        n_seq, tiles_per_seq = 1, t_seq // tm
    side = (n_seq, n_col, CONV_WIDTH - 1, tn)
    y, tail_rows = pl.pallas_call(
        functools.partial(_mix_body, t_seq=t_seq, tiles_per_seq=tiles_per_seq, row_chunk=128),
        grid=(n // tm, n_col),
        in_specs=[
            pl.BlockSpec((tm, d), lambda i, j: (i, 0)),
            pl.BlockSpec((None, 1, d), lambda i, j: (layer, 0, 0)),
            pl.BlockSpec((n_col, tm, tn), lambda i, j: (0, i, 0)),
            pl.BlockSpec((None, d, 5 * tn), lambda i, j: (layer, 0, j)),
            pl.BlockSpec((None, n_col, CONV_WIDTH, tn), lambda i, j: (layer, 0, 0, 0)),
            pl.BlockSpec(side, lambda i, j: (i // tiles_per_seq, 0, 0, 0)),
            pl.BlockSpec((None, tn, d), lambda i, j: (layer, j, 0)),
        ],
        out_specs=[
            pl.BlockSpec((tm, d), lambda i, j: (i, 0)),
            pl.BlockSpec(side, lambda i, j: (i, 0, 0, 0)),
        ],
        out_shape=[
            jax.ShapeDtypeStruct((n, d), F32),
            jax.ShapeDtypeStruct((n // tm * n_seq, n_col, CONV_WIDTH - 1, tn), F32),
        ],
        scratch_shapes=[pltpu.VMEM((tm, d), BF16), pltpu.VMEM((n_col, CONV_WIDTH - 1, tn), F32)],
        compiler_params=_params("arbitrary", "arbitrary"),
        name="mix",
    )(x, g, attn, w5, cw, _col_tiled(prev, tn), wo)
    tail_rows = jnp.swapaxes(tail_rows, 1, 2).reshape(-1, tiles_per_seq, CONV_WIDTH - 1, d)
    return y, tail_rows[:, -1]


def _cross_body(x_ref, g_ref, wq_ref, gq_ref, mk_ref, mv_ref, wo_ref, o_ref, *, t_seq, row_chunk):
    tm = x_ref.shape[0]
    gq = gq_ref[...]
    seq_rows = min(t_seq, tm)
    cm = _tile(seq_rows, row_chunk)
    for r0 in range(0, tm, cm):
        s = r0 // seq_rows
        x = x_ref[r0:r0 + cm, :]
        h = _rms(x, g_ref[...]).astype(BF16)
        q = _dot(h, wq_ref[...])
        heads = []
        for hd in range(MEM_HEADS):
            c0 = hd * MEM_HEAD_DIM
            qh = (_rms(q[:, c0:c0 + MEM_HEAD_DIM], gq) * (MEM_SCALE * LOG2_E)).astype(BF16)
            kh = mk_ref[s, :, c0:c0 + MEM_HEAD_DIM].astype(BF16)
            vh = mv_ref[s, :, c0:c0 + MEM_HEAD_DIM].astype(BF16)
            sc = _dot_nt(qh, kh)
            p = jnp.exp2(sc - jnp.max(sc, axis=-1, keepdims=True))
            l = jnp.sum(p, axis=-1, keepdims=True)
            heads.append((_dot(p.astype(BF16), vh) * (1.0 / l)).astype(BF16))
        o_ref[r0:r0 + cm, :] = x + _dot(jnp.concatenate(heads, axis=1), wo_ref[...])


def _cross(x, g, wq, gq, mem_k, mem_v, wo, layer, t_seq, tm, mem_layer):
    n, d = x.shape
    _, _, n_mem, dm = mem_k.shape
    if tm >= t_seq:
        n_seq = tm // t_seq
        mem_map = lambda i: (mem_layer, i, 0, 0)
    else:
        n_seq, tiles_per_seq = 1, t_seq // tm
        mem_map = lambda i: (mem_layer, i // tiles_per_seq, 0, 0)
    const = dict(pipeline_mode=pl.Buffered(1))
    return pl.pallas_call(
        functools.partial(_cross_body, t_seq=t_seq, row_chunk=tm),
        grid=(n // tm,),
        in_specs=[
            pl.BlockSpec((tm, d), lambda i: (i, 0)),
            pl.BlockSpec((None, 1, d), lambda i: (layer, 0, 0)),
            pl.BlockSpec((None, d, dm), lambda i: (layer, 0, 0), **const),
            pl.BlockSpec((None, 1, MEM_HEAD_DIM), lambda i: (layer, 0, 0)),
            pl.BlockSpec((None, n_seq, n_mem, dm), mem_map),
            pl.BlockSpec((None, n_seq, n_mem, dm), mem_map),
            pl.BlockSpec((None, dm, d), lambda i: (layer, 0, 0), **const),
        ],
        out_specs=pl.BlockSpec((tm, d), lambda i: (i, 0)),
        out_shape=jax.ShapeDtypeStruct((n, d), F32),
        compiler_params=_params("parallel"),
        name="cross",
    )(x, g, wq, gq, mem_k, mem_v, wo)


def _mem_kv_body(m_ref, g_ref, wk_ref, wv_ref, gk_ref, k_ref, v_ref):
    m = _rms(m_ref[...], g_ref[...]).astype(BF16)
    k = _dot(m, wk_ref[...])
    gk = gk_ref[...]
    for hd in range(MEM_HEADS):
        c0 = hd * MEM_HEAD_DIM
        k_ref[:, c0:c0 + MEM_HEAD_DIM] = _rms(k[:, c0:c0 + MEM_HEAD_DIM], gk)
    v_ref[...] = _dot(m, wv_ref[...])


def _mem_kv(mem, g, wk, wv, gk):
    n, d = mem.shape
    n_layers, _, dm = wk.shape
    tm = _tile(n, 512)
    out = jax.ShapeDtypeStruct((n_layers, n, dm), F32)
    return pl.pallas_call(
        _mem_kv_body,
        grid=(n_layers, n // tm),
        in_specs=[
            pl.BlockSpec((tm, d), lambda l, i: (i, 0)),
            pl.BlockSpec((None, 1, d), lambda l, i: (l, 0, 0)),
            pl.BlockSpec((None, d, dm), lambda l, i: (l, 0, 0)),
            pl.BlockSpec((None, d, dm), lambda l, i: (l, 0, 0)),
            pl.BlockSpec((None, 1, MEM_HEAD_DIM), lambda l, i: (l, 0, 0)),
        ],
        out_specs=[
            pl.BlockSpec((None, tm, dm), lambda l, i: (l, i, 0)),
            pl.BlockSpec((None, tm, dm), lambda l, i: (l, i, 0)),
        ],
        out_shape=[out, out],
        compiler_params=_params("parallel", "parallel"),
        name="mem_kv",
    )(mem, g, wk, wv, gk)


def _rope_tables(pos, reps):
    inv = ROPE_BASE ** (-jnp.arange(ROPE_HALF, dtype=F32) / ROPE_HALF)
    ang = pos.astype(F32)[:, None] * inv[None, :]
    cos, sin = jnp.cos(ang), jnp.sin(ang)
    groups = LANE // QK_ROPE
    cos_t = jnp.tile(jnp.concatenate([cos, cos], axis=1), (reps, groups))
    sin_t = jnp.tile(jnp.concatenate([-sin, sin], axis=1), (reps, groups))
    return cos_t, sin_t


def kernel(x_prompt, x_sample, mem_prompt, cache_kv_latent, cache_k_rope, state_conv, cache_mem_k, cache_mem_v, ffn1_norm, ffn1_w_gate, ffn1_w_up, ffn1_w_down, mix_norm, w_in, q_a_norm, w_uq, q_nope_norm, q_rope_norm, kv_a_norm, w_ukv, k_nope_norm, k_rope_norm, conv_w, w_o, cross_norm, mem_norm, w_cq, w_ck, w_cv, cq_norm, ck_norm, w_co, ffn2_norm, ffn2_w_gate, ffn2_w_up, ffn2_w_down):
    bp, tp, d = x_prompt.shape
    bs, ts, _ = x_sample.shape
    n_layers = w_in.shape[0]
    past = cache_kv_latent.shape[2]
    n_mem = mem_prompt.shape[1]
    n_heads = w_uq.shape[-1] // QK_DIM
    dm = MEM_HEADS * MEM_HEAD_DIM

    row = lambda a: a[:, None, :]
    w1gu, w1d = _ffn_weights(ffn1_w_gate, ffn1_w_up, ffn1_w_down)
    w2gu, w2d = _ffn_weights(ffn2_w_gate, ffn2_w_up, ffn2_w_down)
    n_small = Q_LORA + KV_LORA + QK_ROPE
    w_small = jnp.pad(w_in[:, :, :n_small], ((0, 0), (0, 0), (0, SMALL_W - n_small))).astype(BF16)
    w5, conv_w_t, w_o_b = _mix_weights(w_in, conv_w, w_o, n_small)
    uq = w_uq.reshape(n_layers, Q_LORA, n_heads, QK_DIM)
    w_uq_p = jnp.concatenate([uq[..., :QK_NOPE].reshape(n_layers, Q_LORA, n_heads * QK_NOPE),
                              uq[..., QK_NOPE:].reshape(n_layers, Q_LORA, n_heads * QK_ROPE)], axis=-1).astype(BF16)
    ukv = w_ukv.reshape(n_layers, KV_LORA, n_heads, QK_NOPE + V_HEAD)
    w_uk_t = jnp.transpose(ukv[..., :QK_NOPE], (0, 2, 3, 1)).reshape(n_layers, n_heads * QK_NOPE, KV_LORA).astype(BF16)
    w_uv = ukv[..., QK_NOPE:].reshape(n_layers, KV_LORA, n_heads * V_HEAD).astype(BF16)
    w_cq_b, w_ck_b, w_cv_b, w_co_b = w_cq.astype(BF16), w_ck.astype(BF16), w_cv.astype(BF16), w_co.astype(BF16)
    g_qr2 = row(jnp.tile(q_rope_norm, (1, LANE // QK_ROPE)))
    g_kr2 = row(jnp.tile(k_rope_norm, (1, LANE // QK_ROPE)))
    g_kn_col = k_nope_norm[:, :, None]

    tm_p = _tile(tp, 512)
    tm_s = bs * ts
    cos_p, sin_p = _rope_tables(jnp.arange(tp, dtype=jnp.int32), 1)
    cos_s, sin_s = _rope_tables(past + jnp.arange(ts, dtype=jnp.int32), bs)

    tk_s = -(-(past + ts) // 512) * 512
    pad_s = tk_s - past - ts
    cache_krt = jnp.swapaxes(cache_k_rope, -1, -2).astype(BF16)

    mem_k_p, mem_v_p = _mem_kv(mem_prompt.reshape(bp * n_mem, d), row(mem_norm), w_ck_b, w_cv_b, row(ck_norm))
    mem_k_p = mem_k_p.reshape(n_layers, bp, n_mem, dm)
    mem_v_p = mem_v_p.reshape(n_layers, bp, n_mem, dm)
    mem_k_s = cache_mem_k.reshape(n_layers, bs, n_mem, dm)
    mem_v_s = cache_mem_v.reshape(n_layers, bs, n_mem, dm)
    zero_conv = jnp.zeros((bp, CONV_WIDTH - 1, d), F32)

    yp = x_prompt.reshape(bp * tp, d)
    ys = x_sample.reshape(bs * ts, d)
    lat_p, kr_p, cv_p, lat_s, kr_s, cv_s = [], [], [], [], [], []
    for l in range(n_layers):
        def mla(x, cos_t, sin_t, tm):
            return _mla_proj(x, row(mix_norm), w_small, row(q_a_norm), w_uq_p, row(q_nope_norm), g_qr2,
                             row(kv_a_norm), g_kr2, cos_t, sin_t, l, tm)

        yp = _ffn(yp, row(ffn1_norm), w1gu, w1d, l)
        q, nl, nk, nkt = mla(yp, cos_p, sin_p, tm_p)
        kt, v = _kv_proj(nl.reshape(bp, tp, KV_LORA),
                         jnp.transpose(nkt.reshape(QK_ROPE, bp, tp), (1, 0, 2)), w_uk_t, w_uv, g_kn_col, l)
        attn = _attention(q, kt, v, t=tp, past=0, n_valid=tp)
        yp, nc = _mix(yp, row(mix_norm), attn, w5, conv_w_t, zero_conv, w_o_b, l, tp, tm_p)
        yp = _cross(yp, row(cross_norm), w_cq_b, row(cq_norm), mem_k_p, mem_v_p, w_co_b, l, tp, tm_p, l)
        yp = _ffn(yp, row(ffn2_norm), w2gu, w2d, l)
        lat_p.append(nl.reshape(bp, tp, KV_LORA))
        kr_p.append(nk.reshape(bp, tp, QK_ROPE))
        cv_p.append(nc)

        ys = _ffn(ys, row(ffn1_norm), w1gu, w1d, l)
        q, nl, nk, nkt = mla(ys, cos_s, sin_s, tm_s)
        lat_all = jnp.concatenate([cache_kv_latent[l], nl.reshape(bs, ts, KV_LORA),
                                   jnp.zeros((bs, pad_s, KV_LORA), F32)], axis=1)
        krt_all = jnp.concatenate([cache_krt[l], jnp.transpose(nkt.reshape(QK_ROPE, bs, ts), (1, 0, 2)),
                                   jnp.zeros((bs, QK_ROPE, pad_s), BF16)], axis=2)
        kt, v = _kv_proj(lat_all, krt_all, w_uk_t, w_uv, g_kn_col, l)
        attn = _attention(q, kt, v, t=ts, past=past, n_valid=past + ts)
        ys, nc = _mix(ys, row(mix_norm), attn, w5, conv_w_t, state_conv[l], w_o_b, l, ts, tm_s)
        ys = _cross(ys, row(cross_norm), w_cq_b, row(cq_norm), mem_k_s, mem_v_s, w_co_b, l, ts, tm_s, l)
        ys = _ffn(ys, row(ffn2_norm), w2gu, w2d, l)
        lat_s.append(nl.reshape(bs, ts, KV_LORA))
        kr_s.append(nk.reshape(bs, ts, QK_ROPE))
        cv_s.append(nc)

    mem_shape = (n_layers, bp, n_mem, MEM_HEADS, MEM_HEAD_DIM)
    return (yp.reshape(bp, tp, d), ys.reshape(bs, ts, d), jnp.stack(lat_p), jnp.stack(kr_p), jnp.stack(cv_p),
            mem_k_p.reshape(mem_shape), mem_v_p.reshape(mem_shape), jnp.stack(lat_s), jnp.stack(kr_s),
            jnp.stack(cv_s))
```

```python
import functools
import math

import jax
import jax.numpy as jnp
from jax import lax
from jax.experimental import pallas as pl
from jax.experimental.pallas import tpu as pltpu

F32 = jnp.float32
BF16 = jnp.bfloat16

V_HEAD = 128
QK_NOPE = 128
QK_ROPE = 64
ROPE_HALF = QK_ROPE // 2
QK_DIM = QK_NOPE + QK_ROPE
Q_LORA = 768
KV_LORA = 512
CHUNK = 64
ROPE_BASE = 10000.0
MEM_HEADS = 4
MEM_HEAD_DIM = 128
CONV_WIDTH = 3
EPS = 1e-6
MLA_SCALE = 1.0 / math.sqrt(QK_NOPE + QK_ROPE)
MEM_SCALE = 1.0 / math.sqrt(MEM_HEAD_DIM)
LOG2_E = math.log2(math.e)
Q_SCALE = MLA_SCALE * LOG2_E

LANE = 128
V7X_VMEM_BYTES = 64 * 1024 * 1024
VMEM_LIMIT_BYTES = V7X_VMEM_BYTES * 13 // 16
FFN_VMEM_LIMIT_BYTES = V7X_VMEM_BYTES * 15 // 16

FFN_TILE = 512
FFN_ROWS = 1024
FFN_ROW_CHUNK = 512
MIX_TILE = 512
SMALL_W = Q_LORA + KV_LORA + LANE


def _params(*sem, vmem_limit_bytes=VMEM_LIMIT_BYTES):
    return pltpu.CompilerParams(dimension_semantics=sem, vmem_limit_bytes=vmem_limit_bytes)


def _tile(n, target):
    t = min(n, target)
    while n % t:
        t -= 1
    return t


def _rms(x, g):
    ms = jnp.mean(x * x, axis=-1, keepdims=True)
    return x * lax.rsqrt(ms + EPS) * g


def _sigmoid(x):
    return 1.0 / (1.0 + jnp.exp(-x))


def _dot(a, b):
    return jnp.dot(a, b, preferred_element_type=F32)


def _dot_nt(a, b):
    return lax.dot_general(a, b, (((1,), (1,)), ((), ())), preferred_element_type=F32)


def _ffn_body(x_ref, g_ref, wg_ref, wu_ref, wd_ref, o_ref, h_ref):
    @pl.when(pl.program_id(1) == 0)
    def _():
        x = x_ref[...]
        h_ref[...] = _rms(x, g_ref[...]).astype(BF16)
        o_ref[...] = x

    tm = h_ref.shape[0]
    cm = _tile(tm, FFN_ROW_CHUNK)
    for r0 in range(0, tm, cm):
        h = h_ref[r0:r0 + cm, :]
        gate = _dot(h, wg_ref[...])
        up = _dot(h, wu_ref[...])
        act = (gate * _sigmoid(gate) * up * 0.5).astype(BF16)
        o_ref[r0:r0 + cm, :] += _dot(act, wd_ref[...])


def _ffn(x, g, wg, wu, wd, layer):
    n, d = x.shape
    f = wg.shape[-1]
    tm, tf = _tile(n, FFN_ROWS), _tile(f, FFN_TILE)
    return pl.pallas_call(
        _ffn_body,
        grid=(n // tm, f // tf),
        in_specs=[
            pl.BlockSpec((tm, d), lambda i, j: (i, 0)),
            pl.BlockSpec((None, 1, d), lambda i, j: (layer, 0, 0)),
            pl.BlockSpec((None, d, tf), lambda i, j: (layer, 0, j)),
            pl.BlockSpec((None, d, tf), lambda i, j: (layer, 0, j)),
            pl.BlockSpec((None, tf, d), lambda i, j: (layer, j, 0)),
        ],
        out_specs=pl.BlockSpec((tm, d), lambda i, j: (i, 0)),
        out_shape=jax.ShapeDtypeStruct((n, d), F32),
        scratch_shapes=[pltpu.VMEM((tm, d), BF16)],
        compiler_params=_params("parallel", "arbitrary", vmem_limit_bytes=FFN_VMEM_LIMIT_BYTES),
        name="ffn",
    )(x, g, wg, wu, wd)


def _mla_proj_body(x_ref, gmix_ref, ws_ref, gqa_ref, wuq_ref, gqn_ref, gqr_ref, gkva_ref, gkr_ref,
                   cos_ref, sin_ref, q_ref, lat_ref, kr_ref, krt_ref, *, n_heads, row_chunk):
    tm = x_ref.shape[0]
    cm = _tile(tm, row_chunk)
    lane = lax.broadcasted_iota(jnp.int32, (cm, LANE), 1)
    first_half = (lane & ROPE_HALF) == 0
    low_group = lane < QK_ROPE
    gqn = gqn_ref[...]
    gqr = gqr_ref[...]
    rope_base = n_heads * QK_NOPE

    for r0 in range(0, tm, cm):
        rs = slice(r0, r0 + cm)
        h = _rms(x_ref[rs, :], gmix_ref[...]).astype(BF16)
        u = _dot(h, ws_ref[...])
        c_q = u[:, :Q_LORA]
        c_kv = u[:, Q_LORA:Q_LORA + KV_LORA]
        k_r = u[:, Q_LORA + KV_LORA:]

        lat_ref[rs, :] = _rms(c_kv, gkva_ref[...])

        cos_t = cos_ref[rs, :]
        sin_t = sin_ref[rs, :]

        def rope(y):
            swapped = jnp.where(first_half, pltpu.roll(y, LANE - ROPE_HALF, 1), pltpu.roll(y, ROPE_HALF, 1))
            return y * cos_t + swapped * sin_t

        ms = jnp.sum(k_r * k_r, axis=-1, keepdims=True) * (1.0 / QK_ROPE)
        kr = rope(k_r * lax.rsqrt(ms + EPS) * gkr_ref[...])
        kr_ref[rs, :] = kr[:, :QK_ROPE]
        krt_ref[:, rs] = kr.T[:QK_ROPE, :].astype(BF16)

        cq_n = _rms(c_q, gqa_ref[...]).astype(BF16)
        q = _dot(cq_n, wuq_ref[...])
        for hd in range(n_heads):
            qn = _rms(q[:, hd * QK_NOPE:(hd + 1) * QK_NOPE], gqn) * Q_SCALE
            q_ref[hd, rs, :QK_NOPE] = qn.astype(BF16)
        for pair in range(n_heads // 2):
            col = q[:, rope_base + pair * LANE: rope_base + (pair + 1) * LANE]
            sq = col * col
            lo = jnp.sum(jnp.where(low_group, sq, 0.0), axis=-1, keepdims=True)
            hi = jnp.sum(jnp.where(low_group, 0.0, sq), axis=-1, keepdims=True)
            ms = jnp.where(low_group, lo, hi) * (1.0 / QK_ROPE)
            y = (rope(col * lax.rsqrt(ms + EPS) * gqr) * Q_SCALE).astype(BF16)
            q_ref[2 * pair, rs, QK_NOPE:] = y[:, :QK_ROPE]
            q_ref[2 * pair + 1, rs, QK_NOPE:] = y[:, QK_ROPE:]


def _mla_proj(x, gmix, ws, gqa, wuq, gqn, gqr2, gkva, gkr2, cos_t, sin_t, layer, tm):
    n, d = x.shape
    n_heads = wuq.shape[-1] // QK_DIM
    n_pos_tiles = cos_t.shape[0] // tm
    const = dict(pipeline_mode=pl.Buffered(1))
    return pl.pallas_call(
        functools.partial(_mla_proj_body, n_heads=n_heads, row_chunk=256),
        grid=(n // tm,),
        in_specs=[
            pl.BlockSpec((tm, d), lambda i: (i, 0)),
            pl.BlockSpec((None, 1, d), lambda i: (layer, 0, 0)),
            pl.BlockSpec((None, d, SMALL_W), lambda i: (layer, 0, 0), **const),
            pl.BlockSpec((None, 1, Q_LORA), lambda i: (layer, 0, 0)),
            pl.BlockSpec((None, Q_LORA, n_heads * QK_DIM), lambda i: (layer, 0, 0), **const),
            pl.BlockSpec((None, 1, QK_NOPE), lambda i: (layer, 0, 0)),
            pl.BlockSpec((None, 1, LANE), lambda i: (layer, 0, 0)),
            pl.BlockSpec((None, 1, KV_LORA), lambda i: (layer, 0, 0)),
            pl.BlockSpec((None, 1, LANE), lambda i: (layer, 0, 0)),
            pl.BlockSpec((tm, LANE), lambda i: (i % n_pos_tiles, 0)),
            pl.BlockSpec((tm, LANE), lambda i: (i % n_pos_tiles, 0)),
        ],
        out_specs=[
            pl.BlockSpec((n_heads, tm, QK_DIM), lambda i: (0, i, 0)),
            pl.BlockSpec((tm, KV_LORA), lambda i: (i, 0)),
            pl.BlockSpec((tm, QK_ROPE), lambda i: (i, 0)),
            pl.BlockSpec((QK_ROPE, tm), lambda i: (0, i)),
        ],
        out_shape=[
            jax.ShapeDtypeStruct((n_heads, n, QK_DIM), BF16),
            jax.ShapeDtypeStruct((n, KV_LORA), F32),
            jax.ShapeDtypeStruct((n, QK_ROPE), F32),
            jax.ShapeDtypeStruct((QK_ROPE, n), BF16),
        ],
        compiler_params=_params("parallel"),
        name="mla_proj",
    )(x, gmix, ws, gqa, wuq, gqn, gqr2, gkva, gkr2, cos_t, sin_t)


def _kv_proj_body(lat_ref, krt_ref, wukt_ref, wuv_ref, gkn_ref, kt_ref, v_ref, *, n_heads):
    lat = lat_ref[...].astype(BF16)
    knt = _dot_nt(wukt_ref[...], lat)
    krt = krt_ref[...]
    gkn = gkn_ref[...]
    for hd in range(n_heads):
        blk = knt[hd * QK_NOPE:(hd + 1) * QK_NOPE, :]
        ms = jnp.mean(blk * blk, axis=0, keepdims=True)
        kt_ref[hd, :QK_NOPE, :] = (blk * lax.rsqrt(ms + EPS) * gkn).astype(BF16)
        kt_ref[hd, QK_NOPE:, :] = krt
    v = _dot(lat, wuv_ref[...])
    for hd in range(n_heads):
        v_ref[hd] = v[:, hd * V_HEAD:(hd + 1) * V_HEAD].astype(BF16)


def _kv_proj(lat, krt, wukt, wuv, gkn_col, layer):
    b, tk, _ = lat.shape
    n_heads = wuv.shape[-1] // V_HEAD
    tm = _tile(tk, 512)
    const = dict(pipeline_mode=pl.Buffered(1))
    return pl.pallas_call(
        functools.partial(_kv_proj_body, n_heads=n_heads),
        grid=(b, tk // tm),
        in_specs=[
            pl.BlockSpec((None, tm, KV_LORA), lambda i, j: (i, j, 0)),
            pl.BlockSpec((None, QK_ROPE, tm), lambda i, j: (i, 0, j)),
            pl.BlockSpec((None, n_heads * QK_NOPE, KV_LORA), lambda i, j: (layer, 0, 0), **const),
            pl.BlockSpec((None, KV_LORA, n_heads * V_HEAD), lambda i, j: (layer, 0, 0), **const),
            pl.BlockSpec((None, QK_NOPE, 1), lambda i, j: (layer, 0, 0)),
        ],
        out_specs=[
            pl.BlockSpec((None, n_heads, QK_DIM, tm), lambda i, j: (i, 0, 0, j)),
            pl.BlockSpec((None, n_heads, tm, V_HEAD), lambda i, j: (i, 0, j, 0)),
        ],
        out_shape=[
            jax.ShapeDtypeStruct((b, n_heads, QK_DIM, tk), BF16),
            jax.ShapeDtypeStruct((b, n_heads, tk, V_HEAD), BF16),
        ],
        compiler_params=_params("parallel", "parallel"),
        name="kv_proj",
    )(lat, krt, wukt, wuv, gkn_col)


def _attn_body(q_ref, kt_ref, v_ref, o_ref, *, t, tq, past, n_valid, key_tile):
    tk = kt_ref.shape[-1]
    blocks = []
    for qs in range(0, t, tq):
        c_lo = (past + qs) // CHUNK
        c_hi = (past + qs + tq - 1) // CHUNK
        full_end = min((c_lo + 1) * CHUNK, n_valid) // LANE * LANE
        edge_end = min(-(-min((c_hi + 1) * CHUNK, n_valid) // LANE) * LANE, tk)
        tiles = [(lo, min(lo + key_tile, full_end), False) for lo in range(0, full_end, key_tile)]
        tiles += [(lo, min(lo + key_tile, edge_end), True) for lo in range(full_end, edge_end, key_tile)]
        blocks.append((qs, tiles))

    def scores(qs, tile):
        lo, hi, needs_mask = tile
        s = _dot(q_ref[qs:qs + tq, :], kt_ref[:, lo:hi])
        if needs_mask:
            q_chunk = (lax.broadcasted_iota(jnp.int32, (tq, hi - lo), 0) + (past + qs)) >> 6
            k_pos = lax.broadcasted_iota(jnp.int32, (tq, hi - lo), 1) + lo
            s = jnp.where(jnp.logical_and((k_pos >> 6) <= q_chunk, k_pos < n_valid), s, -jnp.inf)
        return s

    def lane_groups(a):
        return [a[:, c:c + LANE] for c in range(0, a.shape[1], LANE)]

    cur = [scores(blocks[0][0], tile) for tile in blocks[0][1]]
    for bi, (qs, tiles) in enumerate(blocks):
        nxt_qs, nxt_tiles = blocks[bi + 1] if bi + 1 < len(blocks) else (None, [])
        m = jnp.max(functools.reduce(jnp.maximum, [g for s in cur for g in lane_groups(s)]),
                    axis=-1, keepdims=True)
        nxt = []
        l_acc = 0.0
        ps = []
        for k, s in enumerate(cur):
            take = len(nxt_tiles) - len(nxt) if k == len(cur) - 1 else 1
            for tile in nxt_tiles[len(nxt):len(nxt) + take]:
                nxt.append(scores(nxt_qs, tile))
            p = jnp.exp2(s - m)
            l_acc = l_acc + functools.reduce(jnp.add, lane_groups(p))
            ps.append(p.astype(BF16))
        p_all = ps[0] if len(ps) == 1 else jnp.concatenate(ps, axis=1)
        o = _dot(p_all, v_ref[tiles[0][0]:tiles[-1][1], :])
        o_ref[qs:qs + tq, :] = o * (1.0 / jnp.sum(l_acc, axis=-1, keepdims=True))
        cur = nxt


def _attention(q, kt, v, *, t, past, n_valid):
    assert CHUNK == 64
    n_heads, n, _ = q.shape
    b, _, _, tk = kt.shape
    tq = _tile(t, 256)
    return pl.pallas_call(
        functools.partial(_attn_body, t=t, tq=tq, past=past, n_valid=n_valid, key_tile=256),
        grid=(b, n_heads),
        in_specs=[
            pl.BlockSpec((None, t, QK_DIM), lambda i, j: (j, i, 0)),
            pl.BlockSpec((None, None, QK_DIM, tk), lambda i, j: (i, j, 0, 0)),
            pl.BlockSpec((None, None, tk, V_HEAD), lambda i, j: (i, j, 0, 0)),
        ],
        out_specs=pl.BlockSpec((t, V_HEAD), lambda i, j: (i, j)),
        out_shape=jax.ShapeDtypeStruct((n, n_heads * V_HEAD), F32),
        compiler_params=_params("parallel", "parallel"),
        name="attention",
    )(q, kt, v)


def _mix_body(x_ref, g_ref, a_ref, wb_ref, wc_ref, wx_ref, wga_ref, wgc_ref, cw_ref, prev_ref, wo_ref,
              o_ref, nc_ref, h_ref, carry_ref, *, t_seq, tiles_per_seq, row_chunk):
    tm, tn = a_ref.shape
    i = pl.program_id(0)
    j = pl.program_id(1)

    @pl.when(j == 0)
    def _():
        x = x_ref[...]
        h_ref[...] = _rms(x, g_ref[...]).astype(BF16)
        o_ref[...] = x

    cw = cw_ref[...]
    if tiles_per_seq == 1:
        befores = [(s * t_seq, prev_ref[s, 0:1, :], prev_ref[s, 1:2, :]) for s in range(tm // t_seq)]
        cm = tm
    else:
        @pl.when(i % tiles_per_seq == 0)
        def _():
            carry_ref[j] = prev_ref[0]

        befores = [(0, carry_ref[j, 0:1, :], carry_ref[j, 1:2, :])]
        cm = _tile(tm, row_chunk)

    for r0 in range(0, tm, cm):
        h = h_ref[r0:r0 + cm, :]
        z = _dot(h, wc_ref[...]) * _dot(h, wx_ref[...])
        rows = lax.broadcasted_iota(jnp.int32, (cm, tn), 0)
        z1 = pltpu.roll(z, 1, 0)
        z2 = pltpu.roll(z, 2, 0)
        for row, p0, p1 in befores:
            z1 = jnp.where(rows == row, p1, z1)
            z2 = jnp.where(rows == row, p0, jnp.where(rows == row + 1, p1, z2))
        befores = [(0, z[cm - 2:cm - 1, :], z[cm - 1:, :])]
        y = cw[0:1, :] * z2 + cw[1:2, :] * z1 + cw[2:3, :] * z
        conv_out = _dot(h, wb_ref[...]) * y
        mixed = (_sigmoid(_dot(h, wga_ref[...])) * a_ref[r0:r0 + cm, :]
                 + _sigmoid(_dot(h, wgc_ref[...])) * conv_out)
        o_ref[r0:r0 + cm, :] += _dot(mixed.astype(BF16), wo_ref[...])
        if tiles_per_seq == 1:
            for s in range(tm // t_seq):
                nc_ref[s] = z[(s + 1) * t_seq - 2:(s + 1) * t_seq, :]
        elif r0 + cm == tm:
            carry_ref[j] = z[cm - 2:, :]
            nc_ref[0] = z[cm - 2:, :]


def _mix(x, g, attn, wb, wc, wx, wga, wgc, cw, prev, wo, layer, t_seq, tm):
    n, d = x.shape
    tn = _tile(d, MIX_TILE)
    assert CONV_WIDTH == 3 and t_seq >= CONV_WIDTH - 1
    if tm >= t_seq:
        assert tm % t_seq == 0
        n_seq, tiles_per_seq = tm // t_seq, 1
        prev_map = lambda i, j: (i, 0, j)
    else:
        assert t_seq % tm == 0
        n_seq, tiles_per_seq = 1, t_seq // tm
        prev_map = lambda i, j: (i // tiles_per_seq, 0, j)
    w_spec = pl.BlockSpec((None, d, tn), lambda i, j: (layer, 0, j))
    y, tail_rows = pl.pallas_call(
        functools.partial(_mix_body, t_seq=t_seq, tiles_per_seq=tiles_per_seq, row_chunk=256),
        grid=(n // tm, d // tn),
        in_specs=[
            pl.BlockSpec((tm, d), lambda i, j: (i, 0)),
            pl.BlockSpec((None, 1, d), lambda i, j: (layer, 0, 0)),
            pl.BlockSpec((tm, tn), lambda i, j: (i, j)),
            w_spec, w_spec, w_spec, w_spec, w_spec,
            pl.BlockSpec((None, CONV_WIDTH, tn), lambda i, j: (layer, 0, j)),
            pl.BlockSpec((n_seq, CONV_WIDTH - 1, tn), prev_map),
            pl.BlockSpec((None, tn, d), lambda i, j: (layer, j, 0)),
        ],
        out_specs=[
            pl.BlockSpec((tm, d), lambda i, j: (i, 0)),
            pl.BlockSpec((n_seq, CONV_WIDTH - 1, tn), lambda i, j: (i, 0, j)),
        ],
        out_shape=[
            jax.ShapeDtypeStruct((n, d), F32),
            jax.ShapeDtypeStruct((n // tm * n_seq, CONV_WIDTH - 1, d), F32),
        ],
        scratch_shapes=[pltpu.VMEM((tm, d), BF16), pltpu.VMEM((d // tn, CONV_WIDTH - 1, tn), F32)],
        compiler_params=_params("arbitrary", "arbitrary"),
        name="mix",
    )(x, g, attn, wb, wc, wx, wga, wgc, cw, prev, wo)
    return y, tail_rows.reshape(-1, tiles_per_seq, CONV_WIDTH - 1, d)[:, -1]


def _cross_body(x_ref, g_ref, wq_ref, gq_ref, mk_ref, mv_ref, wo_ref, o_ref, *, t_seq):
    tm = x_ref.shape[0]
    gq = gq_ref[...]
    cm = min(t_seq, tm)
    for r0 in range(0, tm, cm):
        s = r0 // cm
        x = x_ref[r0:r0 + cm, :]
        h = _rms(x, g_ref[...]).astype(BF16)
        q = _dot(h, wq_ref[...])
        heads = []
        for hd in range(MEM_HEADS):
            c0 = hd * MEM_HEAD_DIM
            qh = (_rms(q[:, c0:c0 + MEM_HEAD_DIM], gq) * (MEM_SCALE * LOG2_E)).astype(BF16)
            kh = mk_ref[s, :, c0:c0 + MEM_HEAD_DIM].astype(BF16)
            vh = mv_ref[s, :, c0:c0 + MEM_HEAD_DIM].astype(BF16)
            sc = _dot_nt(qh, kh)
            p = jnp.exp2(sc - jnp.max(sc, axis=-1, keepdims=True))
            l = jnp.sum(p, axis=-1, keepdims=True)
            heads.append((_dot(p.astype(BF16), vh) * (1.0 / l)).astype(BF16))
        o_ref[r0:r0 + cm, :] = x + _dot(jnp.concatenate(heads, axis=1), wo_ref[...])


def _cross(x, g, wq, gq, mem_k, mem_v, wo, layer, t_seq, tm, mem_layer):
    n, d = x.shape
    _, _, n_mem, dm = mem_k.shape
    if tm >= t_seq:
        n_seq = tm // t_seq
        mem_map = lambda i: (mem_layer, i, 0, 0)
    else:
        n_seq, tiles_per_seq = 1, t_seq // tm
        mem_map = lambda i: (mem_layer, i // tiles_per_seq, 0, 0)
    const = dict(pipeline_mode=pl.Buffered(1))
    return pl.pallas_call(
        functools.partial(_cross_body, t_seq=t_seq),
        grid=(n // tm,),
        in_specs=[
            pl.BlockSpec((tm, d), lambda i: (i, 0)),
            pl.BlockSpec((None, 1, d), lambda i: (layer, 0, 0)),
            pl.BlockSpec((None, d, dm), lambda i: (layer, 0, 0), **const),
            pl.BlockSpec((None, 1, MEM_HEAD_DIM), lambda i: (layer, 0, 0)),
            pl.BlockSpec((None, n_seq, n_mem, dm), mem_map),
            pl.BlockSpec((None, n_seq, n_mem, dm), mem_map),
            pl.BlockSpec((None, dm, d), lambda i: (layer, 0, 0), **const),
        ],
        out_specs=pl.BlockSpec((tm, d), lambda i: (i, 0)),
        out_shape=jax.ShapeDtypeStruct((n, d), F32),
        compiler_params=_params("parallel"),
        name="cross",
    )(x, g, wq, gq, mem_k, mem_v, wo)


def _mem_kv_body(m_ref, g_ref, wk_ref, wv_ref, gk_ref, k_ref, v_ref):
    m = _rms(m_ref[...], g_ref[...]).astype(BF16)
    k = _dot(m, wk_ref[...])
    gk = gk_ref[...]
    for hd in range(MEM_HEADS):
        c0 = hd * MEM_HEAD_DIM
        k_ref[:, c0:c0 + MEM_HEAD_DIM] = _rms(k[:, c0:c0 + MEM_HEAD_DIM], gk)
    v_ref[...] = _dot(m, wv_ref[...])


def _mem_kv(mem, g, wk, wv, gk):
    n, d = mem.shape
    n_layers, _, dm = wk.shape
    tm = _tile(n, 512)
    out = jax.ShapeDtypeStruct((n_layers, n, dm), F32)
    return pl.pallas_call(
        _mem_kv_body,
        grid=(n_layers, n // tm),
        in_specs=[
            pl.BlockSpec((tm, d), lambda l, i: (i, 0)),
            pl.BlockSpec((None, 1, d), lambda l, i: (l, 0, 0)),
            pl.BlockSpec((None, d, dm), lambda l, i: (l, 0, 0)),
            pl.BlockSpec((None, d, dm), lambda l, i: (l, 0, 0)),
            pl.BlockSpec((None, 1, MEM_HEAD_DIM), lambda l, i: (l, 0, 0)),
        ],
        out_specs=[
            pl.BlockSpec((None, tm, dm), lambda l, i: (l, i, 0)),
            pl.BlockSpec((None, tm, dm), lambda l, i: (l, i, 0)),
        ],
        out_shape=[out, out],
        compiler_params=_params("parallel", "parallel"),
        name="mem_kv",
    )(mem, g, wk, wv, gk)


def _rope_tables(pos, reps):
    inv = ROPE_BASE ** (-jnp.arange(ROPE_HALF, dtype=F32) / ROPE_HALF)
    ang = pos.astype(F32)[:, None] * inv[None, :]
    cos, sin = jnp.cos(ang), jnp.sin(ang)
    groups = LANE // QK_ROPE
    cos_t = jnp.tile(jnp.concatenate([cos, cos], axis=1), (reps, groups))
    sin_t = jnp.tile(jnp.concatenate([-sin, sin], axis=1), (reps, groups))
    return cos_t, sin_t


def kernel(x_prompt, x_sample, mem_prompt, cache_kv_latent, cache_k_rope, state_conv, cache_mem_k, cache_mem_v, ffn1_norm, ffn1_w_gate, ffn1_w_up, ffn1_w_down, mix_norm, w_in, q_a_norm, w_uq, q_nope_norm, q_rope_norm, kv_a_norm, w_ukv, k_nope_norm, k_rope_norm, conv_w, w_o, cross_norm, mem_norm, w_cq, w_ck, w_cv, cq_norm, ck_norm, w_co, ffn2_norm, ffn2_w_gate, ffn2_w_up, ffn2_w_down):
    bp, tp, d = x_prompt.shape
    bs, ts, _ = x_sample.shape
    n_layers = w_in.shape[0]
    past = cache_kv_latent.shape[2]
    n_mem = mem_prompt.shape[1]
    n_heads = w_uq.shape[-1] // QK_DIM
    dm = MEM_HEADS * MEM_HEAD_DIM

    row = lambda a: a[:, None, :]
    w1g, w1u, w1d = ffn1_w_gate.astype(BF16), ffn1_w_up.astype(BF16), ffn1_w_down.astype(BF16)
    w2g, w2u, w2d = ffn2_w_gate.astype(BF16), ffn2_w_up.astype(BF16), ffn2_w_down.astype(BF16)
    n_small = Q_LORA + KV_LORA + QK_ROPE
    w_small = jnp.pad(w_in[:, :, :n_small], ((0, 0), (0, 0), (0, SMALL_W - n_small))).astype(BF16)
    w_b, w_c, w_x, w_ga, w_gc = [w_in[:, :, n_small + k * d: n_small + (k + 1) * d].astype(BF16) for k in range(5)]
    uq = w_uq.reshape(n_layers, Q_LORA, n_heads, QK_DIM)
    w_uq_p = jnp.concatenate([uq[..., :QK_NOPE].reshape(n_layers, Q_LORA, n_heads * QK_NOPE),
                              uq[..., QK_NOPE:].reshape(n_layers, Q_LORA, n_heads * QK_ROPE)], axis=-1).astype(BF16)
    ukv = w_ukv.reshape(n_layers, KV_LORA, n_heads, QK_NOPE + V_HEAD)
    w_uk_t = jnp.transpose(ukv[..., :QK_NOPE], (0, 2, 3, 1)).reshape(n_layers, n_heads * QK_NOPE, KV_LORA).astype(BF16)
    w_uv = ukv[..., QK_NOPE:].reshape(n_layers, KV_LORA, n_heads * V_HEAD).astype(BF16)
    w_o_b = w_o.astype(BF16)
    w_cq_b, w_ck_b, w_cv_b, w_co_b = w_cq.astype(BF16), w_ck.astype(BF16), w_cv.astype(BF16), w_co.astype(BF16)
    g_qr2 = row(jnp.tile(q_rope_norm, (1, LANE // QK_ROPE)))
    g_kr2 = row(jnp.tile(k_rope_norm, (1, LANE // QK_ROPE)))
    g_kn_col = k_nope_norm[:, :, None]

    tm_p = _tile(tp, 512)
    tm_s = bs * ts
    cos_p, sin_p = _rope_tables(jnp.arange(tp, dtype=jnp.int32), 1)
    cos_s, sin_s = _rope_tables(past + jnp.arange(ts, dtype=jnp.int32), bs)

    tk_s = -(-(past + ts) // 512) * 512
    pad_s = tk_s - past - ts
    cache_krt = jnp.swapaxes(cache_k_rope, -1, -2).astype(BF16)

    mem_k_p, mem_v_p = _mem_kv(mem_prompt.reshape(bp * n_mem, d), row(mem_norm), w_ck_b, w_cv_b, row(ck_norm))
    mem_k_p = mem_k_p.reshape(n_layers, bp, n_mem, dm)
    mem_v_p = mem_v_p.reshape(n_layers, bp, n_mem, dm)
    mem_k_s = cache_mem_k.reshape(n_layers, bs, n_mem, dm)
    mem_v_s = cache_mem_v.reshape(n_layers, bs, n_mem, dm)
    zero_conv = jnp.zeros((bp, CONV_WIDTH - 1, d), F32)

    yp = x_prompt.reshape(bp * tp, d)
    ys = x_sample.reshape(bs * ts, d)
    lat_p, kr_p, cv_p, lat_s, kr_s, cv_s = [], [], [], [], [], []
    for l in range(n_layers):
        def mla(x, cos_t, sin_t, tm):
            return _mla_proj(x, row(mix_norm), w_small, row(q_a_norm), w_uq_p, row(q_nope_norm), g_qr2,
                             row(kv_a_norm), g_kr2, cos_t, sin_t, l, tm)

        yp = _ffn(yp, row(ffn1_norm), w1g, w1u, w1d, l)
        q, nl, nk, nkt = mla(yp, cos_p, sin_p, tm_p)
        kt, v = _kv_proj(nl.reshape(bp, tp, KV_LORA),
                         jnp.transpose(nkt.reshape(QK_ROPE, bp, tp), (1, 0, 2)), w_uk_t, w_uv, g_kn_col, l)
        attn = _attention(q, kt, v, t=tp, past=0, n_valid=tp)
        yp, nc = _mix(yp, row(mix_norm), attn, w_b, w_c, w_x, w_ga, w_gc, conv_w, zero_conv, w_o_b, l, tp, tm_p)
        yp = _cross(yp, row(cross_norm), w_cq_b, row(cq_norm), mem_k_p, mem_v_p, w_co_b, l, tp, tm_p, l)
        yp = _ffn(yp, row(ffn2_norm), w2g, w2u, w2d, l)
        lat_p.append(nl.reshape(bp, tp, KV_LORA))
        kr_p.append(nk.reshape(bp, tp, QK_ROPE))
        cv_p.append(nc)

        ys = _ffn(ys, row(ffn1_norm), w1g, w1u, w1d, l)
        q, nl, nk, nkt = mla(ys, cos_s, sin_s, tm_s)
        lat_all = jnp.concatenate([cache_kv_latent[l], nl.reshape(bs, ts, KV_LORA),
                                   jnp.zeros((bs, pad_s, KV_LORA), F32)], axis=1)
        krt_all = jnp.concatenate([cache_krt[l], jnp.transpose(nkt.reshape(QK_ROPE, bs, ts), (1, 0, 2)),
                                   jnp.zeros((bs, QK_ROPE, pad_s), BF16)], axis=2)
        kt, v = _kv_proj(lat_all, krt_all, w_uk_t, w_uv, g_kn_col, l)
        attn = _attention(q, kt, v, t=ts, past=past, n_valid=past + ts)
        ys, nc = _mix(ys, row(mix_norm), attn, w_b, w_c, w_x, w_ga, w_gc, conv_w, state_conv[l], w_o_b, l, ts, tm_s)
        ys = _cross(ys, row(cross_norm), w_cq_b, row(cq_norm), mem_k_s, mem_v_s, w_co_b, l, ts, tm_s, l)
        ys = _ffn(ys, row(ffn2_norm), w2g, w2u, w2d, l)
        lat_s.append(nl.reshape(bs, ts, KV_LORA))
        kr_s.append(nk.reshape(bs, ts, QK_ROPE))
        cv_s.append(nc)

    mem_shape = (n_layers, bp, n_mem, MEM_HEADS, MEM_HEAD_DIM)
    return (yp.reshape(bp, tp, d), ys.reshape(bs, ts, d), jnp.stack(lat_p), jnp.stack(kr_p), jnp.stack(cv_p),
            mem_k_p.reshape(mem_shape), mem_v_p.reshape(mem_shape), jnp.stack(lat_s), jnp.stack(kr_s),
            jnp.stack(cv_s))
```

```python
import functools
import math

import jax
import jax.numpy as jnp
from jax import lax
from jax.experimental import pallas as pl
from jax.experimental.pallas import tpu as pltpu

F32 = jnp.float32
BF16 = jnp.bfloat16

V_HEAD = 128
QK_NOPE = 128
QK_ROPE = 64
ROPE_HALF = QK_ROPE // 2
QK_DIM = QK_NOPE + QK_ROPE
Q_LORA = 768
KV_LORA = 512
CHUNK = 64
ROPE_BASE = 10000.0
MEM_HEADS = 4
MEM_HEAD_DIM = 128
CONV_WIDTH = 3
EPS = 1e-6
MLA_SCALE = 1.0 / math.sqrt(QK_NOPE + QK_ROPE)
MEM_SCALE = 1.0 / math.sqrt(MEM_HEAD_DIM)
LOG2_E = math.log2(math.e)
Q_SCALE = MLA_SCALE * LOG2_E

LANE = 128
V7X_VMEM_BYTES = 64 * 1024 * 1024
VMEM_LIMIT_BYTES = V7X_VMEM_BYTES * 13 // 16
FFN_VMEM_LIMIT_BYTES = V7X_VMEM_BYTES * 15 // 16

FFN_TILE = 512
FFN_ROWS = 1024
FFN_ROW_CHUNK = 512
MIX_TILE = 512
ATTN_HEADS_PER_STEP = 2
SMALL_W = Q_LORA + KV_LORA + LANE


def _params(*sem, vmem_limit_bytes=VMEM_LIMIT_BYTES):
    return pltpu.CompilerParams(dimension_semantics=sem, vmem_limit_bytes=vmem_limit_bytes)


def _tile(n, target):
    t = min(n, target)
    while n % t:
        t -= 1
    return t


def _rms(x, g):
    ms = jnp.mean(x * x, axis=-1, keepdims=True)
    return x * lax.rsqrt(ms + EPS) * g


def _sigmoid(x):
    return 1.0 / (1.0 + jnp.exp(-x))


def _dot(a, b):
    return jnp.dot(a, b, preferred_element_type=F32)


def _dot_nt(a, b):
    return lax.dot_general(a, b, (((1,), (1,)), ((), ())), preferred_element_type=F32)


def _ffn_body(x_ref, g_ref, wg_ref, wu_ref, wd_ref, o_ref, h_ref):
    @pl.when(pl.program_id(1) == 0)
    def _():
        x = x_ref[...]
        h_ref[...] = _rms(x, g_ref[...]).astype(BF16)
        o_ref[...] = x

    tm = h_ref.shape[0]
    cm = _tile(tm, FFN_ROW_CHUNK)
    for r0 in range(0, tm, cm):
        h = h_ref[r0:r0 + cm, :]
        gate = _dot(h, wg_ref[...])
        up = _dot(h, wu_ref[...])
        act = (gate * _sigmoid(gate) * up * 0.5).astype(BF16)
        o_ref[r0:r0 + cm, :] += _dot(act, wd_ref[...])


def _ffn(x, g, wg, wu, wd, layer):
    n, d = x.shape
    f = wg.shape[-1]
    tm, tf = _tile(n, FFN_ROWS), _tile(f, FFN_TILE)
    return pl.pallas_call(
        _ffn_body,
        grid=(n // tm, f // tf),
        in_specs=[
            pl.BlockSpec((tm, d), lambda i, j: (i, 0)),
            pl.BlockSpec((None, 1, d), lambda i, j: (layer, 0, 0)),
            pl.BlockSpec((None, d, tf), lambda i, j: (layer, 0, j)),
            pl.BlockSpec((None, d, tf), lambda i, j: (layer, 0, j)),
            pl.BlockSpec((None, tf, d), lambda i, j: (layer, j, 0)),
        ],
        out_specs=pl.BlockSpec((tm, d), lambda i, j: (i, 0)),
        out_shape=jax.ShapeDtypeStruct((n, d), F32),
        scratch_shapes=[pltpu.VMEM((tm, d), BF16)],
        compiler_params=_params("parallel", "arbitrary", vmem_limit_bytes=FFN_VMEM_LIMIT_BYTES),
        name="ffn",
    )(x, g, wg, wu, wd)


def _mla_proj_body(x_ref, gmix_ref, ws_ref, gqa_ref, wuq_ref, gqn_ref, gqr_ref, gkva_ref, gkr_ref,
                   cos_ref, sin_ref, q_ref, lat_ref, kr_ref, krt_ref, *, n_heads, row_chunk):
    tm = x_ref.shape[0]
    cm = _tile(tm, row_chunk)
    lane = lax.broadcasted_iota(jnp.int32, (cm, LANE), 1)
    first_half = (lane & ROPE_HALF) == 0
    low_group = lane < QK_ROPE
    gqn = gqn_ref[...]
    gqr = gqr_ref[...]
    rope_base = n_heads * QK_NOPE

    for r0 in range(0, tm, cm):
        rs = slice(r0, r0 + cm)
        h = _rms(x_ref[rs, :], gmix_ref[...]).astype(BF16)
        u = _dot(h, ws_ref[...])
        c_q = u[:, :Q_LORA]
        c_kv = u[:, Q_LORA:Q_LORA + KV_LORA]
        k_r = u[:, Q_LORA + KV_LORA:]

        lat_ref[rs, :] = _rms(c_kv, gkva_ref[...])

        cos_t = cos_ref[rs, :]
        sin_t = sin_ref[rs, :]

        def rope(y):
            swapped = jnp.where(first_half, pltpu.roll(y, LANE - ROPE_HALF, 1), pltpu.roll(y, ROPE_HALF, 1))
            return y * cos_t + swapped * sin_t

        ms = jnp.sum(k_r * k_r, axis=-1, keepdims=True) * (1.0 / QK_ROPE)
        kr = rope(k_r * lax.rsqrt(ms + EPS) * gkr_ref[...])
        kr_ref[rs, :] = kr[:, :QK_ROPE]
        krt_ref[:, rs] = kr.T[:QK_ROPE, :].astype(BF16)

        cq_n = _rms(c_q, gqa_ref[...]).astype(BF16)
        q = _dot(cq_n, wuq_ref[...])
        for hd in range(n_heads):
            qn = _rms(q[:, hd * QK_NOPE:(hd + 1) * QK_NOPE], gqn) * Q_SCALE
            q_ref[hd, rs, :QK_NOPE] = qn.astype(BF16)
        for pair in range(n_heads // 2):
            col = q[:, rope_base + pair * LANE: rope_base + (pair + 1) * LANE]
            sq = col * col
            lo = jnp.sum(jnp.where(low_group, sq, 0.0), axis=-1, keepdims=True)
            hi = jnp.sum(jnp.where(low_group, 0.0, sq), axis=-1, keepdims=True)
            ms = jnp.where(low_group, lo, hi) * (1.0 / QK_ROPE)
            y = (rope(col * lax.rsqrt(ms + EPS) * gqr) * Q_SCALE).astype(BF16)
            q_ref[2 * pair, rs, QK_NOPE:] = y[:, :QK_ROPE]
            q_ref[2 * pair + 1, rs, QK_NOPE:] = y[:, QK_ROPE:]


def _mla_proj(x, gmix, ws, gqa, wuq, gqn, gqr2, gkva, gkr2, cos_t, sin_t, layer, tm):
    n, d = x.shape
    n_heads = wuq.shape[-1] // QK_DIM
    n_pos_tiles = cos_t.shape[0] // tm
    const = dict(pipeline_mode=pl.Buffered(1))
    return pl.pallas_call(
        functools.partial(_mla_proj_body, n_heads=n_heads, row_chunk=256),
        grid=(n // tm,),
        in_specs=[
            pl.BlockSpec((tm, d), lambda i: (i, 0)),
            pl.BlockSpec((None, 1, d), lambda i: (layer, 0, 0)),
            pl.BlockSpec((None, d, SMALL_W), lambda i: (layer, 0, 0), **const),
            pl.BlockSpec((None, 1, Q_LORA), lambda i: (layer, 0, 0)),
            pl.BlockSpec((None, Q_LORA, n_heads * QK_DIM), lambda i: (layer, 0, 0), **const),
            pl.BlockSpec((None, 1, QK_NOPE), lambda i: (layer, 0, 0)),
            pl.BlockSpec((None, 1, LANE), lambda i: (layer, 0, 0)),
            pl.BlockSpec((None, 1, KV_LORA), lambda i: (layer, 0, 0)),
            pl.BlockSpec((None, 1, LANE), lambda i: (layer, 0, 0)),
            pl.BlockSpec((tm, LANE), lambda i: (i % n_pos_tiles, 0)),
            pl.BlockSpec((tm, LANE), lambda i: (i % n_pos_tiles, 0)),
        ],
        out_specs=[
            pl.BlockSpec((n_heads, tm, QK_DIM), lambda i: (0, i, 0)),
            pl.BlockSpec((tm, KV_LORA), lambda i: (i, 0)),
            pl.BlockSpec((tm, QK_ROPE), lambda i: (i, 0)),
            pl.BlockSpec((QK_ROPE, tm), lambda i: (0, i)),
        ],
        out_shape=[
            jax.ShapeDtypeStruct((n_heads, n, QK_DIM), BF16),
            jax.ShapeDtypeStruct((n, KV_LORA), F32),
            jax.ShapeDtypeStruct((n, QK_ROPE), F32),
            jax.ShapeDtypeStruct((QK_ROPE, n), BF16),
        ],
        compiler_params=_params("parallel"),
        name="mla_proj",
    )(x, gmix, ws, gqa, wuq, gqn, gqr2, gkva, gkr2, cos_t, sin_t)


def _kv_proj_body(lat_ref, krt_ref, wukt_ref, wuv_ref, gkn_ref, kt_ref, v_ref, *, n_heads):
    tm = lat_ref.shape[0]
    cm = _tile(tm, 256)
    gkn = gkn_ref[...]
    for r0 in range(0, tm, cm):
        lat = lat_ref[r0:r0 + cm, :].astype(BF16)
        knt = _dot_nt(wukt_ref[...], lat)
        krt = krt_ref[:, r0:r0 + cm]
        for hd in range(n_heads):
            blk = knt[hd * QK_NOPE:(hd + 1) * QK_NOPE, :]
            ms = jnp.mean(blk * blk, axis=0, keepdims=True)
            kt_ref[hd, :QK_NOPE, r0:r0 + cm] = (blk * lax.rsqrt(ms + EPS) * gkn).astype(BF16)
            kt_ref[hd, QK_NOPE:, r0:r0 + cm] = krt
        v = _dot(lat, wuv_ref[...])
        for hd in range(n_heads):
            v_ref[hd, r0:r0 + cm, :] = v[:, hd * V_HEAD:(hd + 1) * V_HEAD].astype(BF16)


def _kv_proj(lat, krt, wukt, wuv, gkn_col, layer):
    b, tk, _ = lat.shape
    n_heads = wuv.shape[-1] // V_HEAD
    tm = _tile(tk, 512)
    const = dict(pipeline_mode=pl.Buffered(1))
    return pl.pallas_call(
        functools.partial(_kv_proj_body, n_heads=n_heads),
        grid=(b, tk // tm),
        in_specs=[
            pl.BlockSpec((None, tm, KV_LORA), lambda i, j: (i, j, 0)),
            pl.BlockSpec((None, QK_ROPE, tm), lambda i, j: (i, 0, j)),
            pl.BlockSpec((None, n_heads * QK_NOPE, KV_LORA), lambda i, j: (layer, 0, 0), **const),
            pl.BlockSpec((None, KV_LORA, n_heads * V_HEAD), lambda i, j: (layer, 0, 0), **const),
            pl.BlockSpec((None, QK_NOPE, 1), lambda i, j: (layer, 0, 0)),
        ],
        out_specs=[
            pl.BlockSpec((None, n_heads, QK_DIM, tm), lambda i, j: (i, 0, 0, j)),
            pl.BlockSpec((None, n_heads, tm, V_HEAD), lambda i, j: (i, 0, j, 0)),
        ],
        out_shape=[
            jax.ShapeDtypeStruct((b, n_heads, QK_DIM, tk), BF16),
            jax.ShapeDtypeStruct((b, n_heads, tk, V_HEAD), BF16),
        ],
        compiler_params=_params("parallel", "parallel"),
        name="kv_proj",
    )(lat, krt, wukt, wuv, gkn_col)


def _attn_body(q_ref, kt_ref, v_ref, o_ref, *, t, tq, past, n_valid, key_tile):
    n_h, _, tk = kt_ref.shape
    blocks = []
    for hd in range(n_h):
        for qs in range(0, t, tq):
            c_lo = (past + qs) // CHUNK
            c_hi = (past + qs + tq - 1) // CHUNK
            full_end = min((c_lo + 1) * CHUNK, n_valid) // LANE * LANE
            edge_end = min(-(-min((c_hi + 1) * CHUNK, n_valid) // LANE) * LANE, tk)
            tiles = [(lo, min(lo + key_tile, full_end), False) for lo in range(0, full_end, key_tile)]
            tiles += [(lo, min(lo + key_tile, edge_end), True) for lo in range(full_end, edge_end, key_tile)]
            blocks.append((hd, qs, tiles))

    def scores(hd, qs, tile):
        lo, hi, needs_mask = tile
        s = _dot(q_ref[hd, qs:qs + tq, :], kt_ref[hd, :, lo:hi])
        if needs_mask:
            q_chunk = (lax.broadcasted_iota(jnp.int32, (tq, hi - lo), 0) + (past + qs)) >> 6
            k_pos = lax.broadcasted_iota(jnp.int32, (tq, hi - lo), 1) + lo
            s = jnp.where(jnp.logical_and((k_pos >> 6) <= q_chunk, k_pos < n_valid), s, -jnp.inf)
        return s

    def lane_groups(a):
        return [a[:, c:c + LANE] for c in range(0, a.shape[1], LANE)]

    cur = [scores(blocks[0][0], blocks[0][1], tile) for tile in blocks[0][2]]
    for bi, (hd, qs, tiles) in enumerate(blocks):
        nxt_hd, nxt_qs, nxt_tiles = blocks[bi + 1] if bi + 1 < len(blocks) else (None, None, [])
        m = jnp.max(functools.reduce(jnp.maximum, [g for s in cur for g in lane_groups(s)]),
                    axis=-1, keepdims=True)
        nxt = []
        l_acc = 0.0
        ps = []
        for k, s in enumerate(cur):
            upto = len(nxt_tiles) if k == len(cur) - 1 else (k + 1) * len(nxt_tiles) // len(cur)
            for tile in nxt_tiles[len(nxt):upto]:
                nxt.append(scores(nxt_hd, nxt_qs, tile))
            p = jnp.exp2(s - m)
            l_acc = l_acc + functools.reduce(jnp.add, lane_groups(p))
            ps.append(p.astype(BF16))
        p_all = ps[0] if len(ps) == 1 else jnp.concatenate(ps, axis=1)
        o = _dot(p_all, v_ref[hd, tiles[0][0]:tiles[-1][1], :])
        o_ref[qs:qs + tq, hd * V_HEAD:(hd + 1) * V_HEAD] = o * (1.0 / jnp.sum(l_acc, axis=-1, keepdims=True))
        cur = nxt


def _attention(q, kt, v, *, t, past, n_valid):
    assert CHUNK == 64
    n_heads, n, _ = q.shape
    b, _, _, tk = kt.shape
    tq = _tile(t, 256)
    hps = _tile(n_heads, ATTN_HEADS_PER_STEP)
    return pl.pallas_call(
        functools.partial(_attn_body, t=t, tq=tq, past=past, n_valid=n_valid, key_tile=256),
        grid=(b, n_heads // hps),
        in_specs=[
            pl.BlockSpec((hps, t, QK_DIM), lambda i, j: (j, i, 0)),
            pl.BlockSpec((None, hps, QK_DIM, tk), lambda i, j: (i, j, 0, 0)),
            pl.BlockSpec((None, hps, tk, V_HEAD), lambda i, j: (i, j, 0, 0)),
        ],
        out_specs=pl.BlockSpec((t, hps * V_HEAD), lambda i, j: (i, j)),
        out_shape=jax.ShapeDtypeStruct((n, n_heads * V_HEAD), F32),
        compiler_params=_params("parallel", "parallel"),
        name="attention",
    )(q, kt, v)


def _mix_body(x_ref, g_ref, a_ref, wb_ref, wc_ref, wx_ref, wga_ref, wgc_ref, cw_ref, prev_ref, wo_ref,
              o_ref, nc_ref, h_ref, carry_ref, *, t_seq, tiles_per_seq, row_chunk):
    tm, tn = a_ref.shape
    i = pl.program_id(0)
    j = pl.program_id(1)

    @pl.when(j == 0)
    def _():
        x = x_ref[...]
        h_ref[...] = _rms(x, g_ref[...]).astype(BF16)
        o_ref[...] = x

    cw = cw_ref[...]
    if tiles_per_seq == 1:
        befores = [(s * t_seq, prev_ref[s, 0:1, :], prev_ref[s, 1:2, :]) for s in range(tm // t_seq)]
        cm = tm
    else:
        @pl.when(i % tiles_per_seq == 0)
        def _():
            carry_ref[j] = prev_ref[0]

        befores = [(0, carry_ref[j, 0:1, :], carry_ref[j, 1:2, :])]
        cm = _tile(tm, row_chunk)

    for r0 in range(0, tm, cm):
        h = h_ref[r0:r0 + cm, :]
        z = _dot(h, wc_ref[...]) * _dot(h, wx_ref[...])
        rows = lax.broadcasted_iota(jnp.int32, (cm, tn), 0)
        z1 = pltpu.roll(z, 1, 0)
        z2 = pltpu.roll(z, 2, 0)
        for row, p0, p1 in befores:
            z1 = jnp.where(rows == row, p1, z1)
            z2 = jnp.where(rows == row, p0, jnp.where(rows == row + 1, p1, z2))
        befores = [(0, z[cm - 2:cm - 1, :], z[cm - 1:, :])]
        y = cw[0:1, :] * z2 + cw[1:2, :] * z1 + cw[2:3, :] * z
        conv_out = _dot(h, wb_ref[...]) * y
        mixed = (_sigmoid(_dot(h, wga_ref[...])) * a_ref[r0:r0 + cm, :]
                 + _sigmoid(_dot(h, wgc_ref[...])) * conv_out)
        o_ref[r0:r0 + cm, :] += _dot(mixed.astype(BF16), wo_ref[...])
        if tiles_per_seq == 1:
            for s in range(tm // t_seq):
                nc_ref[s] = z[(s + 1) * t_seq - 2:(s + 1) * t_seq, :]
        elif r0 + cm == tm:
            carry_ref[j] = z[cm - 2:, :]
            nc_ref[0] = z[cm - 2:, :]


def _mix(x, g, attn, wb, wc, wx, wga, wgc, cw, prev, wo, layer, t_seq, tm):
    n, d = x.shape
    tn = _tile(d, MIX_TILE)
    assert CONV_WIDTH == 3 and t_seq >= CONV_WIDTH - 1
    if tm >= t_seq:
        assert tm % t_seq == 0
        n_seq, tiles_per_seq = tm // t_seq, 1
        prev_map = lambda i, j: (i, 0, j)
    else:
        assert t_seq % tm == 0
        n_seq, tiles_per_seq = 1, t_seq // tm
        prev_map = lambda i, j: (i // tiles_per_seq, 0, j)
    w_spec = pl.BlockSpec((None, d, tn), lambda i, j: (layer, 0, j))
    y, tail_rows = pl.pallas_call(
        functools.partial(_mix_body, t_seq=t_seq, tiles_per_seq=tiles_per_seq, row_chunk=256),
        grid=(n // tm, d // tn),
        in_specs=[
            pl.BlockSpec((tm, d), lambda i, j: (i, 0)),
            pl.BlockSpec((None, 1, d), lambda i, j: (layer, 0, 0)),
            pl.BlockSpec((tm, tn), lambda i, j: (i, j)),
            w_spec, w_spec, w_spec, w_spec, w_spec,
            pl.BlockSpec((None, CONV_WIDTH, tn), lambda i, j: (layer, 0, j)),
            pl.BlockSpec((n_seq, CONV_WIDTH - 1, tn), prev_map),
            pl.BlockSpec((None, tn, d), lambda i, j: (layer, j, 0)),
        ],
        out_specs=[
            pl.BlockSpec((tm, d), lambda i, j: (i, 0)),
            pl.BlockSpec((n_seq, CONV_WIDTH - 1, tn), lambda i, j: (i, 0, j)),
        ],
        out_shape=[
            jax.ShapeDtypeStruct((n, d), F32),
            jax.ShapeDtypeStruct((n // tm * n_seq, CONV_WIDTH - 1, d), F32),
        ],
        scratch_shapes=[pltpu.VMEM((tm, d), BF16), pltpu.VMEM((d // tn, CONV_WIDTH - 1, tn), F32)],
        compiler_params=_params("arbitrary", "arbitrary"),
        name="mix",
    )(x, g, attn, wb, wc, wx, wga, wgc, cw, prev, wo)
    return y, tail_rows.reshape(-1, tiles_per_seq, CONV_WIDTH - 1, d)[:, -1]


def _cross_body(x_ref, g_ref, wq_ref, gq_ref, mk_ref, mv_ref, wo_ref, o_ref, *, t_seq):
    tm = x_ref.shape[0]
    gq = gq_ref[...]
    cm = min(t_seq, tm)
    for r0 in range(0, tm, cm):
        s = r0 // cm
        x = x_ref[r0:r0 + cm, :]
        h = _rms(x, g_ref[...]).astype(BF16)
        q = _dot(h, wq_ref[...])
        heads = []
        for hd in range(MEM_HEADS):
            c0 = hd * MEM_HEAD_DIM
            qh = (_rms(q[:, c0:c0 + MEM_HEAD_DIM], gq) * (MEM_SCALE * LOG2_E)).astype(BF16)
            kh = mk_ref[s, :, c0:c0 + MEM_HEAD_DIM].astype(BF16)
            vh = mv_ref[s, :, c0:c0 + MEM_HEAD_DIM].astype(BF16)
            sc = _dot_nt(qh, kh)
            p = jnp.exp2(sc - jnp.max(sc, axis=-1, keepdims=True))
            l = jnp.sum(p, axis=-1, keepdims=True)
            heads.append((_dot(p.astype(BF16), vh) * (1.0 / l)).astype(BF16))
        o_ref[r0:r0 + cm, :] = x + _dot(jnp.concatenate(heads, axis=1), wo_ref[...])


def _cross(x, g, wq, gq, mem_k, mem_v, wo, layer, t_seq, tm, mem_layer):
    n, d = x.shape
    _, _, n_mem, dm = mem_k.shape
    if tm >= t_seq:
        n_seq = tm // t_seq
        mem_map = lambda i: (mem_layer, i, 0, 0)
    else:
        n_seq, tiles_per_seq = 1, t_seq // tm
        mem_map = lambda i: (mem_layer, i // tiles_per_seq, 0, 0)
    const = dict(pipeline_mode=pl.Buffered(1))
    return pl.pallas_call(
        functools.partial(_cross_body, t_seq=t_seq),
        grid=(n // tm,),
        in_specs=[
            pl.BlockSpec((tm, d), lambda i: (i, 0)),
            pl.BlockSpec((None, 1, d), lambda i: (layer, 0, 0)),
            pl.BlockSpec((None, d, dm), lambda i: (layer, 0, 0), **const),
            pl.BlockSpec((None, 1, MEM_HEAD_DIM), lambda i: (layer, 0, 0)),
            pl.BlockSpec((None, n_seq, n_mem, dm), mem_map),
            pl.BlockSpec((None, n_seq, n_mem, dm), mem_map),
            pl.BlockSpec((None, dm, d), lambda i: (layer, 0, 0), **const),
        ],
        out_specs=pl.BlockSpec((tm, d), lambda i: (i, 0)),
        out_shape=jax.ShapeDtypeStruct((n, d), F32),
        compiler_params=_params("parallel"),
        name="cross",
    )(x, g, wq, gq, mem_k, mem_v, wo)


def _mem_kv_body(m_ref, g_ref, wk_ref, wv_ref, gk_ref, k_ref, v_ref):
    m = _rms(m_ref[...], g_ref[...]).astype(BF16)
    k = _dot(m, wk_ref[...])
    gk = gk_ref[...]
    for hd in range(MEM_HEADS):
        c0 = hd * MEM_HEAD_DIM
        k_ref[:, c0:c0 + MEM_HEAD_DIM] = _rms(k[:, c0:c0 + MEM_HEAD_DIM], gk)
    v_ref[...] = _dot(m, wv_ref[...])


def _mem_kv(mem, g, wk, wv, gk):
    n, d = mem.shape
    n_layers, _, dm = wk.shape
    tm = _tile(n, 512)
    out = jax.ShapeDtypeStruct((n_layers, n, dm), F32)
    return pl.pallas_call(
        _mem_kv_body,
        grid=(n_layers, n // tm),
        in_specs=[
            pl.BlockSpec((tm, d), lambda l, i: (i, 0)),
            pl.BlockSpec((None, 1, d), lambda l, i: (l, 0, 0)),
            pl.BlockSpec((None, d, dm), lambda l, i: (l, 0, 0)),
            pl.BlockSpec((None, d, dm), lambda l, i: (l, 0, 0)),
            pl.BlockSpec((None, 1, MEM_HEAD_DIM), lambda l, i: (l, 0, 0)),
        ],
        out_specs=[
            pl.BlockSpec((None, tm, dm), lambda l, i: (l, i, 0)),
            pl.BlockSpec((None, tm, dm), lambda l, i: (l, i, 0)),
        ],
        out_shape=[out, out],
        compiler_params=_params("parallel", "parallel"),
        name="mem_kv",
    )(mem, g, wk, wv, gk)


def _rope_tables(pos, reps):
    inv = ROPE_BASE ** (-jnp.arange(ROPE_HALF, dtype=F32) / ROPE_HALF)
    ang = pos.astype(F32)[:, None] * inv[None, :]
    cos, sin = jnp.cos(ang), jnp.sin(ang)
    groups = LANE // QK_ROPE
    cos_t = jnp.tile(jnp.concatenate([cos, cos], axis=1), (reps, groups))
    sin_t = jnp.tile(jnp.concatenate([-sin, sin], axis=1), (reps, groups))
    return cos_t, sin_t


def kernel(x_prompt, x_sample, mem_prompt, cache_kv_latent, cache_k_rope, state_conv, cache_mem_k, cache_mem_v, ffn1_norm, ffn1_w_gate, ffn1_w_up, ffn1_w_down, mix_norm, w_in, q_a_norm, w_uq, q_nope_norm, q_rope_norm, kv_a_norm, w_ukv, k_nope_norm, k_rope_norm, conv_w, w_o, cross_norm, mem_norm, w_cq, w_ck, w_cv, cq_norm, ck_norm, w_co, ffn2_norm, ffn2_w_gate, ffn2_w_up, ffn2_w_down):
    bp, tp, d = x_prompt.shape
    bs, ts, _ = x_sample.shape
    n_layers = w_in.shape[0]
    past = cache_kv_latent.shape[2]
    n_mem = mem_prompt.shape[1]
    n_heads = w_uq.shape[-1] // QK_DIM
    dm = MEM_HEADS * MEM_HEAD_DIM

    row = lambda a: a[:, None, :]
    w1g, w1u, w1d = ffn1_w_gate.astype(BF16), ffn1_w_up.astype(BF16), ffn1_w_down.astype(BF16)
    w2g, w2u, w2d = ffn2_w_gate.astype(BF16), ffn2_w_up.astype(BF16), ffn2_w_down.astype(BF16)
    n_small = Q_LORA + KV_LORA + QK_ROPE
    w_small = jnp.pad(w_in[:, :, :n_small], ((0, 0), (0, 0), (0, SMALL_W - n_small))).astype(BF16)
    w_b, w_c, w_x, w_ga, w_gc = [w_in[:, :, n_small + k * d: n_small + (k + 1) * d].astype(BF16) for k in range(5)]
    uq = w_uq.reshape(n_layers, Q_LORA, n_heads, QK_DIM)
    w_uq_p = jnp.concatenate([uq[..., :QK_NOPE].reshape(n_layers, Q_LORA, n_heads * QK_NOPE),
                              uq[..., QK_NOPE:].reshape(n_layers, Q_LORA, n_heads * QK_ROPE)], axis=-1).astype(BF16)
    ukv = w_ukv.reshape(n_layers, KV_LORA, n_heads, QK_NOPE + V_HEAD)
    w_uk_t = jnp.transpose(ukv[..., :QK_NOPE], (0, 2, 3, 1)).reshape(n_layers, n_heads * QK_NOPE, KV_LORA).astype(BF16)
    w_uv = ukv[..., QK_NOPE:].reshape(n_layers, KV_LORA, n_heads * V_HEAD).astype(BF16)
    w_o_b = w_o.astype(BF16)
    w_cq_b, w_ck_b, w_cv_b, w_co_b = w_cq.astype(BF16), w_ck.astype(BF16), w_cv.astype(BF16), w_co.astype(BF16)
    g_qr2 = row(jnp.tile(q_rope_norm, (1, LANE // QK_ROPE)))
    g_kr2 = row(jnp.tile(k_rope_norm, (1, LANE // QK_ROPE)))
    g_kn_col = k_nope_norm[:, :, None]

    tm_p = _tile(tp, 512)
    tm_s = bs * ts
    cos_p, sin_p = _rope_tables(jnp.arange(tp, dtype=jnp.int32), 1)
    cos_s, sin_s = _rope_tables(past + jnp.arange(ts, dtype=jnp.int32), bs)

    tk_s = -(-(past + ts) // 512) * 512
    pad_s = tk_s - past - ts
    cache_krt = jnp.swapaxes(cache_k_rope, -1, -2).astype(BF16)

    mem_k_p, mem_v_p = _mem_kv(mem_prompt.reshape(bp * n_mem, d), row(mem_norm), w_ck_b, w_cv_b, row(ck_norm))
    mem_k_p = mem_k_p.reshape(n_layers, bp, n_mem, dm)
    mem_v_p = mem_v_p.reshape(n_layers, bp, n_mem, dm)
    mem_k_s = cache_mem_k.reshape(n_layers, bs, n_mem, dm)
    mem_v_s = cache_mem_v.reshape(n_layers, bs, n_mem, dm)
    zero_conv = jnp.zeros((bp, CONV_WIDTH - 1, d), F32)

    yp = x_prompt.reshape(bp * tp, d)
    ys = x_sample.reshape(bs * ts, d)
    lat_p, kr_p, cv_p, lat_s, kr_s, cv_s = [], [], [], [], [], []
    for l in range(n_layers):
        def mla(x, cos_t, sin_t, tm):
            return _mla_proj(x, row(mix_norm), w_small, row(q_a_norm), w_uq_p, row(q_nope_norm), g_qr2,
                             row(kv_a_norm), g_kr2, cos_t, sin_t, l, tm)

        yp = _ffn(yp, row(ffn1_norm), w1g, w1u, w1d, l)
        q, nl, nk, nkt = mla(yp, cos_p, sin_p, tm_p)
        kt, v = _kv_proj(nl.reshape(bp, tp, KV_LORA),
                         jnp.transpose(nkt.reshape(QK_ROPE, bp, tp), (1, 0, 2)), w_uk_t, w_uv, g_kn_col, l)
        attn = _attention(q, kt, v, t=tp, past=0, n_valid=tp)
        yp, nc = _mix(yp, row(mix_norm), attn, w_b, w_c, w_x, w_ga, w_gc, conv_w, zero_conv, w_o_b, l, tp, tm_p)
        yp = _cross(yp, row(cross_norm), w_cq_b, row(cq_norm), mem_k_p, mem_v_p, w_co_b, l, tp, tm_p, l)
        yp = _ffn(yp, row(ffn2_norm), w2g, w2u, w2d, l)
        lat_p.append(nl.reshape(bp, tp, KV_LORA))
        kr_p.append(nk.reshape(bp, tp, QK_ROPE))
        cv_p.append(nc)

        ys = _ffn(ys, row(ffn1_norm), w1g, w1u, w1d, l)
        q, nl, nk, nkt = mla(ys, cos_s, sin_s, tm_s)
        lat_all = jnp.concatenate([cache_kv_latent[l], nl.reshape(bs, ts, KV_LORA),
                                   jnp.zeros((bs, pad_s, KV_LORA), F32)], axis=1)
        krt_all = jnp.concatenate([cache_krt[l], jnp.transpose(nkt.reshape(QK_ROPE, bs, ts), (1, 0, 2)),
                                   jnp.zeros((bs, QK_ROPE, pad_s), BF16)], axis=2)
        kt, v = _kv_proj(lat_all, krt_all, w_uk_t, w_uv, g_kn_col, l)
        attn = _attention(q, kt, v, t=ts, past=past, n_valid=past + ts)
        ys, nc = _mix(ys, row(mix_norm), attn, w_b, w_c, w_x, w_ga, w_gc, conv_w, state_conv[l], w_o_b, l, ts, tm_s)
        ys = _cross(ys, row(cross_norm), w_cq_b, row(cq_norm), mem_k_s, mem_v_s, w_co_b, l, ts, tm_s, l)
        ys = _ffn(ys, row(ffn2_norm), w2g, w2u, w2d, l)
        lat_s.append(nl.reshape(bs, ts, KV_LORA))
        kr_s.append(nk.reshape(bs, ts, QK_ROPE))
        cv_s.append(nc)

    mem_shape = (n_layers, bp, n_mem, MEM_HEADS, MEM_HEAD_DIM)
    return (yp.reshape(bp, tp, d), ys.reshape(bs, ts, d), jnp.stack(lat_p), jnp.stack(kr_p), jnp.stack(cv_p),
            mem_k_p.reshape(mem_shape), mem_v_p.reshape(mem_shape), jnp.stack(lat_s), jnp.stack(kr_s),
            jnp.stack(cv_s))
```

```python
import functools
import math

import jax
import jax.numpy as jnp
from jax import lax
from jax.experimental import pallas as pl
from jax.experimental.pallas import tpu as pltpu

F32 = jnp.float32
BF16 = jnp.bfloat16

V_HEAD = 128
QK_NOPE = 128
QK_ROPE = 64
ROPE_HALF = QK_ROPE // 2
QK_DIM = QK_NOPE + QK_ROPE
Q_LORA = 768
KV_LORA = 512
CHUNK = 64
ROPE_BASE = 10000.0
MEM_HEADS = 4
MEM_HEAD_DIM = 128
CONV_WIDTH = 3
EPS = 1e-6
MLA_SCALE = 1.0 / math.sqrt(QK_NOPE + QK_ROPE)
MEM_SCALE = 1.0 / math.sqrt(MEM_HEAD_DIM)
LOG2_E = math.log2(math.e)
Q_SCALE = MLA_SCALE * LOG2_E

LANE = 128
V7X_VMEM_BYTES = 64 * 1024 * 1024
VMEM_LIMIT_BYTES = V7X_VMEM_BYTES * 13 // 16
FFN_VMEM_LIMIT_BYTES = V7X_VMEM_BYTES * 15 // 16

FFN_TILE = 512
FFN_ROWS = 1024
FFN_ROW_CHUNK = 512
MIX_TILE = 512
ATTN_HEADS_PER_STEP = 2
N_SMALL = Q_LORA + KV_LORA + QK_ROPE
SMALL_W = Q_LORA + KV_LORA + LANE


def _params(*sem, vmem_limit_bytes=VMEM_LIMIT_BYTES):
    return pltpu.CompilerParams(dimension_semantics=sem, vmem_limit_bytes=vmem_limit_bytes)


def _tile(n, target):
    t = min(n, target)
    while n % t:
        t -= 1
    return t


def _rms(x, g):
    ms = jnp.mean(x * x, axis=-1, keepdims=True)
    return x * lax.rsqrt(ms + EPS) * g


def _sigmoid(x):
    return 1.0 / (1.0 + jnp.exp(-x))


def _dot(a, b):
    return jnp.dot(a, b, preferred_element_type=F32)


def _dot_nt(a, b):
    return lax.dot_general(a, b, (((1,), (1,)), ((), ())), preferred_element_type=F32)


def _ffn_body(x_ref, g_ref, wg_ref, wu_ref, wd_ref, o_ref, h_ref):
    @pl.when(pl.program_id(1) == 0)
    def _():
        x = x_ref[...]
        h_ref[...] = _rms(x, g_ref[...]).astype(BF16)
        o_ref[...] = x

    tm = h_ref.shape[0]
    cm = _tile(tm, FFN_ROW_CHUNK)
    for r0 in range(0, tm, cm):
        h = h_ref[r0:r0 + cm, :]
        gate = _dot(h, wg_ref[...])
        up = _dot(h, wu_ref[...])
        act = (gate * _sigmoid(gate) * up * 0.5).astype(BF16)
        o_ref[r0:r0 + cm, :] += _dot(act, wd_ref[...])


def _ffn(x, g, wg, wu, wd, layer):
    n, d = x.shape
    f = wg.shape[-1]
    tm, tf = _tile(n, FFN_ROWS), _tile(f, FFN_TILE)
    return pl.pallas_call(
        _ffn_body,
        grid=(n // tm, f // tf),
        in_specs=[
            pl.BlockSpec((tm, d), lambda i, j: (i, 0)),
            pl.BlockSpec((None, 1, d), lambda i, j: (layer, 0, 0)),
            pl.BlockSpec((None, d, tf), lambda i, j: (layer, 0, j)),
            pl.BlockSpec((None, d, tf), lambda i, j: (layer, 0, j)),
            pl.BlockSpec((None, tf, d), lambda i, j: (layer, j, 0)),
        ],
        out_specs=pl.BlockSpec((tm, d), lambda i, j: (i, 0)),
        out_shape=jax.ShapeDtypeStruct((n, d), F32),
        scratch_shapes=[pltpu.VMEM((tm, d), BF16)],
        compiler_params=_params("parallel", "arbitrary", vmem_limit_bytes=FFN_VMEM_LIMIT_BYTES),
        name="ffn",
    )(x, g, wg, wu, wd)


def _mla_proj_body(x_ref, gmix_ref, ws_ref, gqa_ref, wuq_ref, gqn_ref, gqr_ref, gkva_ref, gkr_ref,
                   cos_ref, sin_ref, *rest, n_heads, row_chunk):
    q_ref, lat_ref, kr_ref, krt_ref = rest[-4:]
    tm = x_ref.shape[0]
    cm = _tile(tm, row_chunk)
    lane = lax.broadcasted_iota(jnp.int32, (cm, LANE), 1)
    first_half = (lane & ROPE_HALF) == 0
    low_group = lane < QK_ROPE
    gqn = gqn_ref[...]
    gqr = gqr_ref[...]
    rope_base = n_heads * QK_NOPE

    for r0 in range(0, tm, cm):
        rs = slice(r0, r0 + cm)
        h = _rms(x_ref[rs, :], gmix_ref[...]).astype(BF16)
        u = _dot(h, ws_ref[...])
        c_q = u[:, :Q_LORA]
        c_kv = u[:, Q_LORA:Q_LORA + KV_LORA]
        k_r = u[:, Q_LORA + KV_LORA:]

        lat_ref[rs, :] = _rms(c_kv, gkva_ref[...])

        cos_t = cos_ref[rs, :]
        sin_t = sin_ref[rs, :]

        def rope(y):
            swapped = jnp.where(first_half, pltpu.roll(y, LANE - ROPE_HALF, 1), pltpu.roll(y, ROPE_HALF, 1))
            return y * cos_t + swapped * sin_t

        ms = jnp.sum(k_r * k_r, axis=-1, keepdims=True) * (1.0 / QK_ROPE)
        kr = rope(k_r * lax.rsqrt(ms + EPS) * gkr_ref[...])
        kr_ref[rs, :] = kr[:, :QK_ROPE]
        krt_ref[:, rs] = kr.T[:QK_ROPE, :].astype(BF16)

        cq_n = _rms(c_q, gqa_ref[...]).astype(BF16)
        q = _dot(cq_n, wuq_ref[...])
        for hd in range(n_heads):
            qn = _rms(q[:, hd * QK_NOPE:(hd + 1) * QK_NOPE], gqn) * Q_SCALE
            q_ref[hd, rs, :QK_NOPE] = qn.astype(BF16)
        for pair in range(n_heads // 2):
            col = q[:, rope_base + pair * LANE: rope_base + (pair + 1) * LANE]
            sq = col * col
            lo = jnp.sum(jnp.where(low_group, sq, 0.0), axis=-1, keepdims=True)
            hi = jnp.sum(jnp.where(low_group, 0.0, sq), axis=-1, keepdims=True)
            ms = jnp.where(low_group, lo, hi) * (1.0 / QK_ROPE)
            y = (rope(col * lax.rsqrt(ms + EPS) * gqr) * Q_SCALE).astype(BF16)
            q_ref[2 * pair, rs, QK_NOPE:] = y[:, :QK_ROPE]
            q_ref[2 * pair + 1, rs, QK_NOPE:] = y[:, QK_ROPE:]


def _mla_proj(x, gmix, ws, gqa, wuq, gqn, gqr2, gkva, gkr2, cos_t, sin_t, layer, tm, n_layers, slabs):
    n, d = x.shape
    alias_specs = [] if slabs is None else [pl.BlockSpec(memory_space=pl.ANY)] * 2
    n_in = 11
    n_heads = wuq.shape[-1] // QK_DIM
    n_pos_tiles = cos_t.shape[0] // tm
    const = dict(pipeline_mode=pl.Buffered(1))
    return pl.pallas_call(
        functools.partial(_mla_proj_body, n_heads=n_heads, row_chunk=256),
        grid=(n // tm,),
        in_specs=[
            pl.BlockSpec((tm, d), lambda i: (i, 0)),
            pl.BlockSpec((None, 1, d), lambda i: (layer, 0, 0)),
            pl.BlockSpec((None, d, SMALL_W), lambda i: (layer, 0, 0), **const),
            pl.BlockSpec((None, 1, Q_LORA), lambda i: (layer, 0, 0)),
            pl.BlockSpec((None, Q_LORA, n_heads * QK_DIM), lambda i: (layer, 0, 0), **const),
            pl.BlockSpec((None, 1, QK_NOPE), lambda i: (layer, 0, 0)),
            pl.BlockSpec((None, 1, LANE), lambda i: (layer, 0, 0)),
            pl.BlockSpec((None, 1, KV_LORA), lambda i: (layer, 0, 0)),
            pl.BlockSpec((None, 1, LANE), lambda i: (layer, 0, 0)),
            pl.BlockSpec((tm, LANE), lambda i: (i % n_pos_tiles, 0)),
            pl.BlockSpec((tm, LANE), lambda i: (i % n_pos_tiles, 0)),
            *alias_specs,
        ],
        out_specs=[
            pl.BlockSpec((n_heads, tm, QK_DIM), lambda i: (0, i, 0)),
            pl.BlockSpec((None, tm, KV_LORA), lambda i: (layer, i, 0)),
            pl.BlockSpec((None, tm, QK_ROPE), lambda i: (layer, i, 0)),
            pl.BlockSpec((QK_ROPE, tm), lambda i: (0, i)),
        ],
        out_shape=[
            jax.ShapeDtypeStruct((n_heads, n, QK_DIM), BF16),
            jax.ShapeDtypeStruct((n_layers, n, KV_LORA), F32),
            jax.ShapeDtypeStruct((n_layers, n, QK_ROPE), F32),
            jax.ShapeDtypeStruct((QK_ROPE, n), BF16),
        ],
        input_output_aliases={} if slabs is None else {n_in: 1, n_in + 1: 2},
        compiler_params=_params("parallel"),
        name="mla_proj",
    )(x, gmix, ws, gqa, wuq, gqn, gqr2, gkva, gkr2, cos_t, sin_t, *(slabs or ()))


def _kv_proj_body(lat_ref, krt_ref, wukt_ref, wuv_ref, gkn_ref, kt_ref, v_ref, *, n_heads):
    tm = lat_ref.shape[0]
    cm = _tile(tm, 256)
    gkn = gkn_ref[...]
    for r0 in range(0, tm, cm):
        lat = lat_ref[r0:r0 + cm, :].astype(BF16)
        knt = _dot_nt(wukt_ref[...], lat)
        krt = krt_ref[:, r0:r0 + cm]
        for hd in range(n_heads):
            blk = knt[hd * QK_NOPE:(hd + 1) * QK_NOPE, :]
            ms = jnp.mean(blk * blk, axis=0, keepdims=True)
            kt_ref[hd, :QK_NOPE, r0:r0 + cm] = (blk * lax.rsqrt(ms + EPS) * gkn).astype(BF16)
            kt_ref[hd, QK_NOPE:, r0:r0 + cm] = krt
        v = _dot(lat, wuv_ref[...])
        for hd in range(n_heads):
            v_ref[hd, r0:r0 + cm, :] = v[:, hd * V_HEAD:(hd + 1) * V_HEAD].astype(BF16)


def _kv_proj(lat, lat_layer, krt, wukt, wuv, gkn_col, layer):
    _, b, tk, _ = lat.shape
    n_heads = wuv.shape[-1] // V_HEAD
    tm = _tile(tk, 512)
    const = dict(pipeline_mode=pl.Buffered(1))
    return pl.pallas_call(
        functools.partial(_kv_proj_body, n_heads=n_heads),
        grid=(b, tk // tm),
        in_specs=[
            pl.BlockSpec((None, None, tm, KV_LORA), lambda i, j: (lat_layer, i, j, 0)),
            pl.BlockSpec((None, QK_ROPE, tm), lambda i, j: (i, 0, j)),
            pl.BlockSpec((None, n_heads * QK_NOPE, KV_LORA), lambda i, j: (layer, 0, 0), **const),
            pl.BlockSpec((None, KV_LORA, n_heads * V_HEAD), lambda i, j: (layer, 0, 0), **const),
            pl.BlockSpec((None, QK_NOPE, 1), lambda i, j: (layer, 0, 0)),
        ],
        out_specs=[
            pl.BlockSpec((None, n_heads, QK_DIM, tm), lambda i, j: (i, 0, 0, j)),
            pl.BlockSpec((None, n_heads, tm, V_HEAD), lambda i, j: (i, 0, j, 0)),
        ],
        out_shape=[
            jax.ShapeDtypeStruct((b, n_heads, QK_DIM, tk), BF16),
            jax.ShapeDtypeStruct((b, n_heads, tk, V_HEAD), BF16),
        ],
        compiler_params=_params("parallel", "parallel"),
        name="kv_proj",
    )(lat, krt, wukt, wuv, gkn_col)


def _attn_body(q_ref, kt_ref, v_ref, o_ref, *, t, tq, past, n_valid, key_tile):
    n_h, _, tk = kt_ref.shape
    blocks = []
    for hd in range(n_h):
        for qs in range(0, t, tq):
            c_lo = (past + qs) // CHUNK
            c_hi = (past + qs + tq - 1) // CHUNK
            full_end = min((c_lo + 1) * CHUNK, n_valid) // LANE * LANE
            edge_end = min(-(-min((c_hi + 1) * CHUNK, n_valid) // LANE) * LANE, tk)
            tiles = [(lo, min(lo + key_tile, full_end), False) for lo in range(0, full_end, key_tile)]
            tiles += [(lo, min(lo + key_tile, edge_end), True) for lo in range(full_end, edge_end, key_tile)]
            blocks.append((hd, qs, tiles))

    def scores(hd, qs, tile):
        lo, hi, needs_mask = tile
        s = _dot(q_ref[hd, qs:qs + tq, :], kt_ref[hd, :, lo:hi])
        if needs_mask:
            q_chunk = (lax.broadcasted_iota(jnp.int32, (tq, hi - lo), 0) + (past + qs)) >> 6
            k_pos = lax.broadcasted_iota(jnp.int32, (tq, hi - lo), 1) + lo
            s = jnp.where(jnp.logical_and((k_pos >> 6) <= q_chunk, k_pos < n_valid), s, -jnp.inf)
        return s

    def lane_groups(a):
        return [a[:, c:c + LANE] for c in range(0, a.shape[1], LANE)]

    cur = [scores(blocks[0][0], blocks[0][1], tile) for tile in blocks[0][2]]
    for bi, (hd, qs, tiles) in enumerate(blocks):
        nxt_hd, nxt_qs, nxt_tiles = blocks[bi + 1] if bi + 1 < len(blocks) else (None, None, [])
        m = jnp.max(functools.reduce(jnp.maximum, [g for s in cur for g in lane_groups(s)]),
                    axis=-1, keepdims=True)
        nxt = []
        l_acc = 0.0
        ps = []
        for k, s in enumerate(cur):
            upto = len(nxt_tiles) if k == len(cur) - 1 else (k + 1) * len(nxt_tiles) // len(cur)
            for tile in nxt_tiles[len(nxt):upto]:
                nxt.append(scores(nxt_hd, nxt_qs, tile))
            p = jnp.exp2(s - m)
            l_acc = l_acc + functools.reduce(jnp.add, lane_groups(p))
            ps.append(p.astype(BF16))
        p_all = ps[0] if len(ps) == 1 else jnp.concatenate(ps, axis=1)
        o = _dot(p_all, v_ref[hd, tiles[0][0]:tiles[-1][1], :])
        o_ref[qs:qs + tq, hd * V_HEAD:(hd + 1) * V_HEAD] = o * (1.0 / jnp.sum(l_acc, axis=-1, keepdims=True))
        cur = nxt


def _attention(q, kt, v, *, t, past, n_valid):
    assert CHUNK == 64
    n_heads, n, _ = q.shape
    b, _, _, tk = kt.shape
    tq = _tile(t, 256)
    hps = _tile(n_heads, ATTN_HEADS_PER_STEP)
    return pl.pallas_call(
        functools.partial(_attn_body, t=t, tq=tq, past=past, n_valid=n_valid, key_tile=256),
        grid=(b, n_heads // hps),
        in_specs=[
            pl.BlockSpec((hps, t, QK_DIM), lambda i, j: (j, i, 0)),
            pl.BlockSpec((None, hps, QK_DIM, tk), lambda i, j: (i, j, 0, 0)),
            pl.BlockSpec((None, hps, tk, V_HEAD), lambda i, j: (i, j, 0, 0)),
        ],
        out_specs=pl.BlockSpec((t, hps * V_HEAD), lambda i, j: (i, j)),
        out_shape=jax.ShapeDtypeStruct((n, n_heads * V_HEAD), F32),
        compiler_params=_params("parallel", "parallel"),
        name="attention",
    )(q, kt, v)


def _mix_body(x_ref, g_ref, a_ref, wb_ref, wc_ref, wx_ref, wga_ref, wgc_ref, cw_ref, prev_ref, wo_ref,
              o_ref, nc_ref, h_ref, carry_ref, *, t_seq, tiles_per_seq, row_chunk):
    tm, tn = a_ref.shape
    i = pl.program_id(0)
    j = pl.program_id(1)

    @pl.when(j == 0)
    def _():
        x = x_ref[...]
        h_ref[...] = _rms(x, g_ref[...]).astype(BF16)
        o_ref[...] = x

    cw = cw_ref[...]
    if tiles_per_seq == 1:
        befores = [(s * t_seq, prev_ref[s, 0:1, :], prev_ref[s, 1:2, :]) for s in range(tm // t_seq)]
        cm = tm
    else:
        @pl.when(i % tiles_per_seq == 0)
        def _():
            carry_ref[j] = prev_ref[0]

        befores = [(0, carry_ref[j, 0:1, :], carry_ref[j, 1:2, :])]
        cm = _tile(tm, row_chunk)

    for r0 in range(0, tm, cm):
        h = h_ref[r0:r0 + cm, :]
        z = _dot(h, wc_ref[...]) * _dot(h, wx_ref[...])
        rows = lax.broadcasted_iota(jnp.int32, (cm, tn), 0)
        z1 = pltpu.roll(z, 1, 0)
        z2 = pltpu.roll(z, 2, 0)
        for row, p0, p1 in befores:
            z1 = jnp.where(rows == row, p1, z1)
            z2 = jnp.where(rows == row, p0, jnp.where(rows == row + 1, p1, z2))
        befores = [(0, z[cm - 2:cm - 1, :], z[cm - 1:, :])]
        y = cw[0:1, :] * z2 + cw[1:2, :] * z1 + cw[2:3, :] * z
        conv_out = _dot(h, wb_ref[...]) * y
        mixed = (_sigmoid(_dot(h, wga_ref[...])) * a_ref[r0:r0 + cm, :]
                 + _sigmoid(_dot(h, wgc_ref[...])) * conv_out)
        o_ref[r0:r0 + cm, :] += _dot(mixed.astype(BF16), wo_ref[...])
        if tiles_per_seq == 1:
            for s in range(tm // t_seq):
                nc_ref[s] = z[(s + 1) * t_seq - 2:(s + 1) * t_seq, :]
        elif r0 + cm == tm:
            carry_ref[j] = z[cm - 2:, :]
            nc_ref[0] = z[cm - 2:, :]


def _mix(x, g, attn, w5, cw, prev, wo, layer, t_seq, tm):
    n, d = x.shape
    tn = _tile(d, MIX_TILE)
    assert CONV_WIDTH == 3 and t_seq >= CONV_WIDTH - 1
    if tm >= t_seq:
        assert tm % t_seq == 0
        n_seq, tiles_per_seq = tm // t_seq, 1
        prev_map = lambda i, j: (i, 0, j)
    else:
        assert t_seq % tm == 0
        n_seq, tiles_per_seq = 1, t_seq // tm
        prev_map = lambda i, j: (i // tiles_per_seq, 0, j)
    w_specs = [pl.BlockSpec((None, d, tn), lambda i, j, c0=k * d // tn: (layer, 0, c0 + j)) for k in range(5)]
    y, tail_rows = pl.pallas_call(
        functools.partial(_mix_body, t_seq=t_seq, tiles_per_seq=tiles_per_seq, row_chunk=256),
        grid=(n // tm, d // tn),
        in_specs=[
            pl.BlockSpec((tm, d), lambda i, j: (i, 0)),
            pl.BlockSpec((None, 1, d), lambda i, j: (layer, 0, 0)),
            pl.BlockSpec((tm, tn), lambda i, j: (i, j)),
            *w_specs,
            pl.BlockSpec((None, CONV_WIDTH, tn), lambda i, j: (layer, 0, j)),
            pl.BlockSpec((n_seq, CONV_WIDTH - 1, tn), prev_map),
            pl.BlockSpec((None, tn, d), lambda i, j: (layer, j, 0)),
        ],
        out_specs=[
            pl.BlockSpec((tm, d), lambda i, j: (i, 0)),
            pl.BlockSpec((n_seq, CONV_WIDTH - 1, tn), lambda i, j: (i, 0, j)),
        ],
        out_shape=[
            jax.ShapeDtypeStruct((n, d), F32),
            jax.ShapeDtypeStruct((n // tm * n_seq, CONV_WIDTH - 1, d), F32),
        ],
        scratch_shapes=[pltpu.VMEM((tm, d), BF16), pltpu.VMEM((d // tn, CONV_WIDTH - 1, tn), F32)],
        compiler_params=_params("arbitrary", "arbitrary"),
        name="mix",
    )(x, g, attn, w5, w5, w5, w5, w5, cw, prev, wo)
    return y, tail_rows.reshape(-1, tiles_per_seq, CONV_WIDTH - 1, d)[:, -1]


def _cross_body(x_ref, g_ref, wq_ref, gq_ref, mk_ref, mv_ref, wo_ref, o_ref, *, t_seq):
    tm = x_ref.shape[0]
    gq = gq_ref[...]
    cm = min(t_seq, tm)
    for r0 in range(0, tm, cm):
        s = r0 // cm
        x = x_ref[r0:r0 + cm, :]
        h = _rms(x, g_ref[...]).astype(BF16)
        q = _dot(h, wq_ref[...])
        heads = []
        for hd in range(MEM_HEADS):
            c0 = hd * MEM_HEAD_DIM
            qh = (_rms(q[:, c0:c0 + MEM_HEAD_DIM], gq) * (MEM_SCALE * LOG2_E)).astype(BF16)
            kh = mk_ref[s, :, c0:c0 + MEM_HEAD_DIM].astype(BF16)
            vh = mv_ref[s, :, c0:c0 + MEM_HEAD_DIM].astype(BF16)
            sc = _dot_nt(qh, kh)
            p = jnp.exp2(sc - jnp.max(sc, axis=-1, keepdims=True))
            l = jnp.sum(p, axis=-1, keepdims=True)
            heads.append((_dot(p.astype(BF16), vh) * (1.0 / l)).astype(BF16))
        o_ref[r0:r0 + cm, :] = x + _dot(jnp.concatenate(heads, axis=1), wo_ref[...])


def _cross(x, g, wq, gq, mem_k, mem_v, wo, layer, t_seq, tm, mem_layer):
    n, d = x.shape
    _, _, n_mem, dm = mem_k.shape
    if tm >= t_seq:
        n_seq = tm // t_seq
        mem_map = lambda i: (mem_layer, i, 0, 0)
    else:
        n_seq, tiles_per_seq = 1, t_seq // tm
        mem_map = lambda i: (mem_layer, i // tiles_per_seq, 0, 0)
    const = dict(pipeline_mode=pl.Buffered(1))
    return pl.pallas_call(
        functools.partial(_cross_body, t_seq=t_seq),
        grid=(n // tm,),
        in_specs=[
            pl.BlockSpec((tm, d), lambda i: (i, 0)),
            pl.BlockSpec((None, 1, d), lambda i: (layer, 0, 0)),
            pl.BlockSpec((None, d, dm), lambda i: (layer, 0, 0), **const),
            pl.BlockSpec((None, 1, MEM_HEAD_DIM), lambda i: (layer, 0, 0)),
            pl.BlockSpec((None, n_seq, n_mem, dm), mem_map),
            pl.BlockSpec((None, n_seq, n_mem, dm), mem_map),
            pl.BlockSpec((None, dm, d), lambda i: (layer, 0, 0), **const),
        ],
        out_specs=pl.BlockSpec((tm, d), lambda i: (i, 0)),
        out_shape=jax.ShapeDtypeStruct((n, d), F32),
        compiler_params=_params("parallel"),
        name="cross",
    )(x, g, wq, gq, mem_k, mem_v, wo)


def _mem_kv_body(m_ref, g_ref, wk_ref, wv_ref, gk_ref, k_ref, v_ref):
    m = _rms(m_ref[...], g_ref[...]).astype(BF16)
    k = _dot(m, wk_ref[...])
    gk = gk_ref[...]
    for hd in range(MEM_HEADS):
        c0 = hd * MEM_HEAD_DIM
        k_ref[:, c0:c0 + MEM_HEAD_DIM] = _rms(k[:, c0:c0 + MEM_HEAD_DIM], gk)
    v_ref[...] = _dot(m, wv_ref[...])


def _mem_kv(mem, g, wk, wv, gk):
    n, d = mem.shape
    n_layers, _, dm = wk.shape
    tm = _tile(n, 512)
    out = jax.ShapeDtypeStruct((n_layers, n, dm), F32)
    return pl.pallas_call(
        _mem_kv_body,
        grid=(n_layers, n // tm),
        in_specs=[
            pl.BlockSpec((tm, d), lambda l, i: (i, 0)),
            pl.BlockSpec((None, 1, d), lambda l, i: (l, 0, 0)),
            pl.BlockSpec((None, d, dm), lambda l, i: (l, 0, 0)),
            pl.BlockSpec((None, d, dm), lambda l, i: (l, 0, 0)),
            pl.BlockSpec((None, 1, MEM_HEAD_DIM), lambda l, i: (l, 0, 0)),
        ],
        out_specs=[
            pl.BlockSpec((None, tm, dm), lambda l, i: (l, i, 0)),
            pl.BlockSpec((None, tm, dm), lambda l, i: (l, i, 0)),
        ],
        out_shape=[out, out],
        compiler_params=_params("parallel", "parallel"),
        name="mem_kv",
    )(mem, g, wk, wv, gk)


def _rope_tables(pos, reps):
    inv = ROPE_BASE ** (-jnp.arange(ROPE_HALF, dtype=F32) / ROPE_HALF)
    ang = pos.astype(F32)[:, None] * inv[None, :]
    cos, sin = jnp.cos(ang), jnp.sin(ang)
    groups = LANE // QK_ROPE
    cos_t = jnp.tile(jnp.concatenate([cos, cos], axis=1), (reps, groups))
    sin_t = jnp.tile(jnp.concatenate([-sin, sin], axis=1), (reps, groups))
    return cos_t, sin_t


def kernel(x_prompt, x_sample, mem_prompt, cache_kv_latent, cache_k_rope, state_conv, cache_mem_k, cache_mem_v, ffn1_norm, ffn1_w_gate, ffn1_w_up, ffn1_w_down, mix_norm, w_in, q_a_norm, w_uq, q_nope_norm, q_rope_norm, kv_a_norm, w_ukv, k_nope_norm, k_rope_norm, conv_w, w_o, cross_norm, mem_norm, w_cq, w_ck, w_cv, cq_norm, ck_norm, w_co, ffn2_norm, ffn2_w_gate, ffn2_w_up, ffn2_w_down):
    bp, tp, d = x_prompt.shape
    bs, ts, _ = x_sample.shape
    n_layers = w_in.shape[0]
    past = cache_kv_latent.shape[2]
    n_mem = mem_prompt.shape[1]
    n_heads = w_uq.shape[-1] // QK_DIM
    dm = MEM_HEADS * MEM_HEAD_DIM

    row = lambda a: a[:, None, :]
    w1g, w1u, w1d = ffn1_w_gate.astype(BF16), ffn1_w_up.astype(BF16), ffn1_w_down.astype(BF16)
    w2g, w2u, w2d = ffn2_w_gate.astype(BF16), ffn2_w_up.astype(BF16), ffn2_w_down.astype(BF16)
    w_small = jnp.pad(w_in[:, :, :N_SMALL], ((0, 0), (0, 0), (0, SMALL_W - N_SMALL))).astype(BF16)
    w5 = w_in[:, :, N_SMALL:].astype(BF16)
    uq = w_uq.reshape(n_layers, Q_LORA, n_heads, QK_DIM)
    w_uq_p = jnp.concatenate([uq[..., :QK_NOPE].reshape(n_layers, Q_LORA, n_heads * QK_NOPE),
                              uq[..., QK_NOPE:].reshape(n_layers, Q_LORA, n_heads * QK_ROPE)], axis=-1).astype(BF16)
    ukv = w_ukv.reshape(n_layers, KV_LORA, n_heads, QK_NOPE + V_HEAD)
    w_uk_t = jnp.transpose(ukv[..., :QK_NOPE], (0, 2, 3, 1)).reshape(n_layers, n_heads * QK_NOPE, KV_LORA).astype(BF16)
    w_uv = ukv[..., QK_NOPE:].reshape(n_layers, KV_LORA, n_heads * V_HEAD).astype(BF16)
    w_o_b = w_o.astype(BF16)
    w_cq_b, w_ck_b, w_cv_b, w_co_b = w_cq.astype(BF16), w_ck.astype(BF16), w_cv.astype(BF16), w_co.astype(BF16)
    g_qr2 = row(jnp.tile(q_rope_norm, (1, LANE // QK_ROPE)))
    g_kr2 = row(jnp.tile(k_rope_norm, (1, LANE // QK_ROPE)))
    g_kn_col = k_nope_norm[:, :, None]

    tm_p = _tile(tp, 512)
    tm_s = bs * ts
    cos_p, sin_p = _rope_tables(jnp.arange(tp, dtype=jnp.int32), 1)
    cos_s, sin_s = _rope_tables(past + jnp.arange(ts, dtype=jnp.int32), bs)

    tk_s = -(-(past + ts) // 512) * 512
    pad_s = tk_s - past - ts
    cache_krt = jnp.swapaxes(cache_k_rope, -1, -2).astype(BF16)

    mem_k_p, mem_v_p = _mem_kv(mem_prompt.reshape(bp * n_mem, d), row(mem_norm), w_ck_b, w_cv_b, row(ck_norm))
    mem_k_p = mem_k_p.reshape(n_layers, bp, n_mem, dm)
    mem_v_p = mem_v_p.reshape(n_layers, bp, n_mem, dm)
    mem_k_s = cache_mem_k.reshape(n_layers, bs, n_mem, dm)
    mem_v_s = cache_mem_v.reshape(n_layers, bs, n_mem, dm)
    zero_conv = jnp.zeros((bp, CONV_WIDTH - 1, d), F32)

    yp = x_prompt.reshape(bp * tp, d)
    ys = x_sample.reshape(bs * ts, d)
    cv_p, cv_s = [], []
    slabs_p = slabs_s = None
    for l in range(n_layers):
        def mla(x, cos_t, sin_t, tm, slabs):
            return _mla_proj(x, row(mix_norm), w_small, row(q_a_norm), w_uq_p, row(q_nope_norm), g_qr2,
                             row(kv_a_norm), g_kr2, cos_t, sin_t, l, tm, n_layers, slabs)

        yp = _ffn(yp, row(ffn1_norm), w1g, w1u, w1d, l)
        q, lat_all_p, kr_all_p, nkt = mla(yp, cos_p, sin_p, tm_p, slabs_p)
        slabs_p = (lat_all_p, kr_all_p)
        kt, v = _kv_proj(lat_all_p.reshape(n_layers, bp, tp, KV_LORA), l,
                         jnp.transpose(nkt.reshape(QK_ROPE, bp, tp), (1, 0, 2)), w_uk_t, w_uv, g_kn_col, l)
        attn = _attention(q, kt, v, t=tp, past=0, n_valid=tp)
        yp, nc = _mix(yp, row(mix_norm), attn, w5, conv_w, zero_conv, w_o_b, l, tp, tm_p)
        yp = _cross(yp, row(cross_norm), w_cq_b, row(cq_norm), mem_k_p, mem_v_p, w_co_b, l, tp, tm_p, l)
        yp = _ffn(yp, row(ffn2_norm), w2g, w2u, w2d, l)
        cv_p.append(nc)

        ys = _ffn(ys, row(ffn1_norm), w1g, w1u, w1d, l)
        q, lat_all_s, kr_all_s, nkt = mla(ys, cos_s, sin_s, tm_s, slabs_s)
        slabs_s = (lat_all_s, kr_all_s)
        lat_all = jnp.concatenate([cache_kv_latent[l], lat_all_s[l].reshape(bs, ts, KV_LORA),
                                   jnp.zeros((bs, pad_s, KV_LORA), F32)], axis=1)
        krt_all = jnp.concatenate([cache_krt[l], jnp.transpose(nkt.reshape(QK_ROPE, bs, ts), (1, 0, 2)),
                                   jnp.zeros((bs, QK_ROPE, pad_s), BF16)], axis=2)
        kt, v = _kv_proj(lat_all[None], 0, krt_all, w_uk_t, w_uv, g_kn_col, l)
        attn = _attention(q, kt, v, t=ts, past=past, n_valid=past + ts)
        ys, nc = _mix(ys, row(mix_norm), attn, w5, conv_w, state_conv[l], w_o_b, l, ts, tm_s)
        ys = _cross(ys, row(cross_norm), w_cq_b, row(cq_norm), mem_k_s, mem_v_s, w_co_b, l, ts, tm_s, l)
        ys = _ffn(ys, row(ffn2_norm), w2g, w2u, w2d, l)
        cv_s.append(nc)

    mem_shape = (n_layers, bp, n_mem, MEM_HEADS, MEM_HEAD_DIM)
    return (yp.reshape(bp, tp, d), ys.reshape(bs, ts, d),
            slabs_p[0].reshape(n_layers, bp, tp, KV_LORA), slabs_p[1].reshape(n_layers, bp, tp, QK_ROPE),
            jnp.stack(cv_p), mem_k_p.reshape(mem_shape), mem_v_p.reshape(mem_shape),
            slabs_s[0].reshape(n_layers, bs, ts, KV_LORA), slabs_s[1].reshape(n_layers, bs, ts, QK_ROPE),
            jnp.stack(cv_s))
```

```python
import functools
import math

import jax
import jax.numpy as jnp
from jax import lax
from jax.experimental import pallas as pl
from jax.experimental.pallas import tpu as pltpu

F32 = jnp.float32
BF16 = jnp.bfloat16

V_HEAD = 128
QK_NOPE = 128
QK_ROPE = 64
ROPE_HALF = QK_ROPE // 2
QK_DIM = QK_NOPE + QK_ROPE
Q_LORA = 768
KV_LORA = 512
CHUNK = 64
ROPE_BASE = 10000.0
MEM_HEADS = 4
MEM_HEAD_DIM = 128
CONV_WIDTH = 3
EPS = 1e-6
MLA_SCALE = 1.0 / math.sqrt(QK_NOPE + QK_ROPE)
MEM_SCALE = 1.0 / math.sqrt(MEM_HEAD_DIM)
LOG2_E = math.log2(math.e)
Q_SCALE = MLA_SCALE * LOG2_E

LANE = 128
V7X_VMEM_BYTES = 64 * 1024 * 1024
VMEM_LIMIT_BYTES = V7X_VMEM_BYTES * 13 // 16
FFN_VMEM_LIMIT_BYTES = V7X_VMEM_BYTES * 15 // 16

FFN_TILE = 512
FFN_ROWS = 1024
FFN_ROW_CHUNK = 512
MIX_TILE = 512
ATTN_HEADS_PER_STEP = 2
N_SMALL = Q_LORA + KV_LORA + QK_ROPE
SMALL_W = Q_LORA + KV_LORA + LANE


def _params(*sem, vmem_limit_bytes=VMEM_LIMIT_BYTES):
    return pltpu.CompilerParams(dimension_semantics=sem, vmem_limit_bytes=vmem_limit_bytes)


def _tile(n, target):
    t = min(n, target)
    while n % t:
        t -= 1
    return t


def _rms(x, g):
    ms = jnp.mean(x * x, axis=-1, keepdims=True)
    return x * lax.rsqrt(ms + EPS) * g


def _sigmoid(x):
    return 1.0 / (1.0 + jnp.exp(-x))


def _dot(a, b):
    return jnp.dot(a, b, preferred_element_type=F32)


def _dot_nt(a, b):
    return lax.dot_general(a, b, (((1,), (1,)), ((), ())), preferred_element_type=F32)


def _ffn_body(x_ref, g_ref, wg_ref, wu_ref, wd_ref, o_ref, h_ref):
    tm = h_ref.shape[0]
    cm = _tile(tm, FFN_ROW_CHUNK)

    def down(h):
        gate = _dot(h, wg_ref[...])
        up = _dot(h, wu_ref[...])
        act = (gate * _sigmoid(gate) * up * 0.5).astype(BF16)
        return _dot(act, wd_ref[...])

    @pl.when(pl.program_id(1) == 0)
    def _():
        for r0 in range(0, tm, cm):
            x = x_ref[r0:r0 + cm, :]
            h = _rms(x, g_ref[...]).astype(BF16)
            h_ref[r0:r0 + cm, :] = h
            o_ref[r0:r0 + cm, :] = x + down(h)

    @pl.when(pl.program_id(1) != 0)
    def _():
        for r0 in range(0, tm, cm):
            o_ref[r0:r0 + cm, :] += down(h_ref[r0:r0 + cm, :])


def _ffn(x, g, wg, wu, wd, layer):
    n, d = x.shape
    f = wg.shape[-1]
    tm, tf = _tile(n, FFN_ROWS), _tile(f, FFN_TILE)
    return pl.pallas_call(
        _ffn_body,
        grid=(n // tm, f // tf),
        in_specs=[
            pl.BlockSpec((tm, d), lambda i, j: (i, 0)),
            pl.BlockSpec((None, 1, d), lambda i, j: (layer, 0, 0)),
            pl.BlockSpec((None, d, tf), lambda i, j: (layer, 0, j)),
            pl.BlockSpec((None, d, tf), lambda i, j: (layer, 0, j)),
            pl.BlockSpec((None, tf, d), lambda i, j: (layer, j, 0)),
        ],
        out_specs=pl.BlockSpec((tm, d), lambda i, j: (i, 0)),
        out_shape=jax.ShapeDtypeStruct((n, d), F32),
        scratch_shapes=[pltpu.VMEM((tm, d), BF16)],
        compiler_params=_params("parallel", "arbitrary", vmem_limit_bytes=FFN_VMEM_LIMIT_BYTES),
        name="ffn",
    )(x, g, wg, wu, wd)


def _mla_proj_body(x_ref, gmix_ref, ws_ref, gqa_ref, wuq_ref, gqn_ref, gqr_ref, gkva_ref, gkr_ref,
                   cos_ref, sin_ref, *rest, n_heads, row_chunk):
    q_ref, lat_ref, kr_ref, krt_ref = rest[-4:]
    tm = x_ref.shape[0]
    cm = _tile(tm, row_chunk)
    lane = lax.broadcasted_iota(jnp.int32, (cm, LANE), 1)
    first_half = (lane & ROPE_HALF) == 0
    low_group = lane < QK_ROPE
    gqn = gqn_ref[...]
    gqr = gqr_ref[...]
    rope_base = n_heads * QK_NOPE

    for r0 in range(0, tm, cm):
        rs = slice(r0, r0 + cm)
        h = _rms(x_ref[rs, :], gmix_ref[...]).astype(BF16)
        u = _dot(h, ws_ref[...])
        c_q = u[:, :Q_LORA]
        c_kv = u[:, Q_LORA:Q_LORA + KV_LORA]
        k_r = u[:, Q_LORA + KV_LORA:]

        lat_ref[rs, :] = _rms(c_kv, gkva_ref[...])

        cos_t = cos_ref[rs, :]
        sin_t = sin_ref[rs, :]

        def rope(y):
            swapped = jnp.where(first_half, pltpu.roll(y, LANE - ROPE_HALF, 1), pltpu.roll(y, ROPE_HALF, 1))
            return y * cos_t + swapped * sin_t

        ms = jnp.sum(k_r * k_r, axis=-1, keepdims=True) * (1.0 / QK_ROPE)
        kr = rope(k_r * lax.rsqrt(ms + EPS) * gkr_ref[...])
        kr_ref[rs, :] = kr[:, :QK_ROPE]
        krt_ref[:, rs] = kr.T[:QK_ROPE, :].astype(BF16)

        cq_n = _rms(c_q, gqa_ref[...]).astype(BF16)
        q = _dot(cq_n, wuq_ref[...])
        for hd in range(n_heads):
            qn = _rms(q[:, hd * QK_NOPE:(hd + 1) * QK_NOPE], gqn) * Q_SCALE
            q_ref[hd, rs, :QK_NOPE] = qn.astype(BF16)
        for pair in range(n_heads // 2):
            col = q[:, rope_base + pair * LANE: rope_base + (pair + 1) * LANE]
            sq = col * col
            lo = jnp.sum(jnp.where(low_group, sq, 0.0), axis=-1, keepdims=True)
            hi = jnp.sum(jnp.where(low_group, 0.0, sq), axis=-1, keepdims=True)
            ms = jnp.where(low_group, lo, hi) * (1.0 / QK_ROPE)
            y = (rope(col * lax.rsqrt(ms + EPS) * gqr) * Q_SCALE).astype(BF16)
            q_ref[2 * pair, rs, QK_NOPE:] = y[:, :QK_ROPE]
            q_ref[2 * pair + 1, rs, QK_NOPE:] = y[:, QK_ROPE:]


def _mla_proj(x, gmix, ws, gqa, wuq, gqn, gqr2, gkva, gkr2, cos_t, sin_t, layer, tm, n_layers, slabs):
    n, d = x.shape
    alias_specs = [] if slabs is None else [pl.BlockSpec(memory_space=pl.ANY)] * 2
    n_in = 11
    n_heads = wuq.shape[-1] // QK_DIM
    n_pos_tiles = cos_t.shape[0] // tm
    const = dict(pipeline_mode=pl.Buffered(1))
    return pl.pallas_call(
        functools.partial(_mla_proj_body, n_heads=n_heads, row_chunk=256),
        grid=(n // tm,),
        in_specs=[
            pl.BlockSpec((tm, d), lambda i: (i, 0)),
            pl.BlockSpec((None, 1, d), lambda i: (layer, 0, 0)),
            pl.BlockSpec((None, d, SMALL_W), lambda i: (layer, 0, 0), **const),
            pl.BlockSpec((None, 1, Q_LORA), lambda i: (layer, 0, 0)),
            pl.BlockSpec((None, Q_LORA, n_heads * QK_DIM), lambda i: (layer, 0, 0), **const),
            pl.BlockSpec((None, 1, QK_NOPE), lambda i: (layer, 0, 0)),
            pl.BlockSpec((None, 1, LANE), lambda i: (layer, 0, 0)),
            pl.BlockSpec((None, 1, KV_LORA), lambda i: (layer, 0, 0)),
            pl.BlockSpec((None, 1, LANE), lambda i: (layer, 0, 0)),
            pl.BlockSpec((tm, LANE), lambda i: (i % n_pos_tiles, 0)),
            pl.BlockSpec((tm, LANE), lambda i: (i % n_pos_tiles, 0)),
            *alias_specs,
        ],
        out_specs=[
            pl.BlockSpec((n_heads, tm, QK_DIM), lambda i: (0, i, 0)),
            pl.BlockSpec((None, tm, KV_LORA), lambda i: (layer, i, 0)),
            pl.BlockSpec((None, tm, QK_ROPE), lambda i: (layer, i, 0)),
            pl.BlockSpec((QK_ROPE, tm), lambda i: (0, i)),
        ],
        out_shape=[
            jax.ShapeDtypeStruct((n_heads, n, QK_DIM), BF16),
            jax.ShapeDtypeStruct((n_layers, n, KV_LORA), F32),
            jax.ShapeDtypeStruct((n_layers, n, QK_ROPE), F32),
            jax.ShapeDtypeStruct((QK_ROPE, n), BF16),
        ],
        input_output_aliases={} if slabs is None else {n_in: 1, n_in + 1: 2},
        compiler_params=_params("parallel"),
        name="mla_proj",
    )(x, gmix, ws, gqa, wuq, gqn, gqr2, gkva, gkr2, cos_t, sin_t, *(slabs or ()))


def _kv_proj_body(lat_ref, krt_ref, wukt_ref, wuv_ref, gkn_ref, kt_ref, v_ref, *, n_heads):
    tm = lat_ref.shape[0]
    cm = _tile(tm, 256)
    gkn = gkn_ref[...]
    for r0 in range(0, tm, cm):
        lat = lat_ref[r0:r0 + cm, :].astype(BF16)
        knt = _dot_nt(wukt_ref[...], lat)
        krt = krt_ref[:, r0:r0 + cm]
        for hd in range(n_heads):
            blk = knt[hd * QK_NOPE:(hd + 1) * QK_NOPE, :]
            ms = jnp.mean(blk * blk, axis=0, keepdims=True)
            kt_ref[hd, :QK_NOPE, r0:r0 + cm] = (blk * lax.rsqrt(ms + EPS) * gkn).astype(BF16)
            kt_ref[hd, QK_NOPE:, r0:r0 + cm] = krt
        v = _dot(lat, wuv_ref[...])
        for hd in range(n_heads):
            v_ref[hd, r0:r0 + cm, :] = v[:, hd * V_HEAD:(hd + 1) * V_HEAD].astype(BF16)


def _kv_proj(lat, lat_layer, krt, wukt, wuv, gkn_col, layer):
    _, b, tk, _ = lat.shape
    n_heads = wuv.shape[-1] // V_HEAD
    tm = _tile(tk, 512)
    const = dict(pipeline_mode=pl.Buffered(1))
    return pl.pallas_call(
        functools.partial(_kv_proj_body, n_heads=n_heads),
        grid=(b, tk // tm),
        in_specs=[
            pl.BlockSpec((None, None, tm, KV_LORA), lambda i, j: (lat_layer, i, j, 0)),
            pl.BlockSpec((None, QK_ROPE, tm), lambda i, j: (i, 0, j)),
            pl.BlockSpec((None, n_heads * QK_NOPE, KV_LORA), lambda i, j: (layer, 0, 0), **const),
            pl.BlockSpec((None, KV_LORA, n_heads * V_HEAD), lambda i, j: (layer, 0, 0), **const),
            pl.BlockSpec((None, QK_NOPE, 1), lambda i, j: (layer, 0, 0)),
        ],
        out_specs=[
            pl.BlockSpec((None, n_heads, QK_DIM, tm), lambda i, j: (i, 0, 0, j)),
            pl.BlockSpec((None, n_heads, tm, V_HEAD), lambda i, j: (i, 0, j, 0)),
        ],
        out_shape=[
            jax.ShapeDtypeStruct((b, n_heads, QK_DIM, tk), BF16),
            jax.ShapeDtypeStruct((b, n_heads, tk, V_HEAD), BF16),
        ],
        compiler_params=_params("parallel", "parallel"),
        name="kv_proj",
    )(lat, krt, wukt, wuv, gkn_col)


def _attn_body(q_ref, kt_ref, v_ref, o_ref, *, t, tq, past, n_valid, key_tile):
    n_h, _, tk = kt_ref.shape
    blocks = []
    for hd in range(n_h):
        for qs in range(0, t, tq):
            c_lo = (past + qs) // CHUNK
            c_hi = (past + qs + tq - 1) // CHUNK
            full_end = min((c_lo + 1) * CHUNK, n_valid) // LANE * LANE
            edge_end = min(-(-min((c_hi + 1) * CHUNK, n_valid) // LANE) * LANE, tk)
            tiles = [(lo, min(lo + key_tile, full_end), False) for lo in range(0, full_end, key_tile)]
            tiles += [(lo, min(lo + key_tile, edge_end), True) for lo in range(full_end, edge_end, key_tile)]
            blocks.append((hd, qs, tiles))

    def scores(hd, qs, tile):
        lo, hi, needs_mask = tile
        s = _dot(q_ref[hd, qs:qs + tq, :], kt_ref[hd, :, lo:hi])
        if needs_mask:
            q_chunk = (lax.broadcasted_iota(jnp.int32, (tq, hi - lo), 0) + (past + qs)) >> 6
            k_pos = lax.broadcasted_iota(jnp.int32, (tq, hi - lo), 1) + lo
            s = jnp.where(jnp.logical_and((k_pos >> 6) <= q_chunk, k_pos < n_valid), s, -jnp.inf)
        return s

    def lane_groups(a):
        return [a[:, c:c + LANE] for c in range(0, a.shape[1], LANE)]

    cur = [scores(blocks[0][0], blocks[0][1], tile) for tile in blocks[0][2]]
    for bi, (hd, qs, tiles) in enumerate(blocks):
        nxt_hd, nxt_qs, nxt_tiles = blocks[bi + 1] if bi + 1 < len(blocks) else (None, None, [])
        m = jnp.max(functools.reduce(jnp.maximum, [g for s in cur for g in lane_groups(s)]),
                    axis=-1, keepdims=True)
        nxt = []
        l_acc = 0.0
        ps = []
        for k, s in enumerate(cur):
            upto = len(nxt_tiles) if k == len(cur) - 1 else (k + 1) * len(nxt_tiles) // len(cur)
            for tile in nxt_tiles[len(nxt):upto]:
                nxt.append(scores(nxt_hd, nxt_qs, tile))
            p = jnp.exp2(s - m)
            l_acc = l_acc + functools.reduce(jnp.add, lane_groups(p))
            ps.append(p.astype(BF16))
        p_all = ps[0] if len(ps) == 1 else jnp.concatenate(ps, axis=1)
        o = _dot(p_all, v_ref[hd, tiles[0][0]:tiles[-1][1], :])
        o_ref[qs:qs + tq, hd * V_HEAD:(hd + 1) * V_HEAD] = o * (1.0 / jnp.sum(l_acc, axis=-1, keepdims=True))
        cur = nxt


def _attention(q, kt, v, *, t, past, n_valid):
    assert CHUNK == 64
    n_heads, n, _ = q.shape
    b, _, _, tk = kt.shape
    tq = _tile(t, 256)
    hps = _tile(n_heads, ATTN_HEADS_PER_STEP)
    return pl.pallas_call(
        functools.partial(_attn_body, t=t, tq=tq, past=past, n_valid=n_valid, key_tile=256),
        grid=(b, n_heads // hps),
        in_specs=[
            pl.BlockSpec((hps, t, QK_DIM), lambda i, j: (j, i, 0)),
            pl.BlockSpec((None, hps, QK_DIM, tk), lambda i, j: (i, j, 0, 0)),
            pl.BlockSpec((None, hps, tk, V_HEAD), lambda i, j: (i, j, 0, 0)),
        ],
        out_specs=pl.BlockSpec((t, hps * V_HEAD), lambda i, j: (i, j)),
        out_shape=jax.ShapeDtypeStruct((n, n_heads * V_HEAD), F32),
        compiler_params=_params("parallel", "parallel"),
        name="attention",
    )(q, kt, v)


def _mix_body(x_ref, g_ref, a_ref, wb_ref, wc_ref, wx_ref, wga_ref, wgc_ref, cw_ref, prev_ref, wo_ref,
              o_ref, nc_ref, h_ref, carry_ref, *, t_seq, tiles_per_seq, row_chunk):
    tm, tn = a_ref.shape
    i = pl.program_id(0)
    j = pl.program_id(1)

    cw = cw_ref[...]
    if tiles_per_seq == 1:
        first_befores = [(s * t_seq, prev_ref[s, 0:1, :], prev_ref[s, 1:2, :]) for s in range(tm // t_seq)]
        cm = tm
    else:
        @pl.when(i % tiles_per_seq == 0)
        def _():
            carry_ref[j] = prev_ref[0]

        first_befores = [(0, carry_ref[j, 0:1, :], carry_ref[j, 1:2, :])]
        cm = _tile(tm, row_chunk)

    def run(first_col_tile):
        befores = first_befores
        for r0 in range(0, tm, cm):
            if first_col_tile:
                x = x_ref[r0:r0 + cm, :]
                h = _rms(x, g_ref[...]).astype(BF16)
                h_ref[r0:r0 + cm, :] = h
            else:
                h = h_ref[r0:r0 + cm, :]
            z = _dot(h, wc_ref[...]) * _dot(h, wx_ref[...])
            rows = lax.broadcasted_iota(jnp.int32, (cm, tn), 0)
            z1 = pltpu.roll(z, 1, 0)
            z2 = pltpu.roll(z, 2, 0)
            for row, p0, p1 in befores:
                z1 = jnp.where(rows == row, p1, z1)
                z2 = jnp.where(rows == row, p0, jnp.where(rows == row + 1, p1, z2))
            befores = [(0, z[cm - 2:cm - 1, :], z[cm - 1:, :])]
            y = cw[0:1, :] * z2 + cw[1:2, :] * z1 + cw[2:3, :] * z
            conv_out = _dot(h, wb_ref[...]) * y
            mixed = (_sigmoid(_dot(h, wga_ref[...])) * a_ref[r0:r0 + cm, :]
                     + _sigmoid(_dot(h, wgc_ref[...])) * conv_out)
            out = _dot(mixed.astype(BF16), wo_ref[...])
            if first_col_tile:
                o_ref[r0:r0 + cm, :] = x + out
            else:
                o_ref[r0:r0 + cm, :] += out
            if tiles_per_seq == 1:
                for s in range(tm // t_seq):
                    nc_ref[s] = z[(s + 1) * t_seq - 2:(s + 1) * t_seq, :]
            elif r0 + cm == tm:
                carry_ref[j] = z[cm - 2:, :]
                nc_ref[0] = z[cm - 2:, :]

    pl.when(j == 0)(functools.partial(run, True))
    pl.when(j != 0)(functools.partial(run, False))


def _mix(x, g, attn, w5, cw, prev, wo, layer, t_seq, tm):
    n, d = x.shape
    tn = _tile(d, MIX_TILE)
    assert CONV_WIDTH == 3 and t_seq >= CONV_WIDTH - 1
    if tm >= t_seq:
        assert tm % t_seq == 0
        n_seq, tiles_per_seq = tm // t_seq, 1
        prev_map = lambda i, j: (i, 0, j)
    else:
        assert t_seq % tm == 0
        n_seq, tiles_per_seq = 1, t_seq // tm
        prev_map = lambda i, j: (i // tiles_per_seq, 0, j)
    w_specs = [pl.BlockSpec((None, d, tn), lambda i, j, c0=k * d // tn: (layer, 0, c0 + j)) for k in range(5)]
    y, tail_rows = pl.pallas_call(
        functools.partial(_mix_body, t_seq=t_seq, tiles_per_seq=tiles_per_seq, row_chunk=256),
        grid=(n // tm, d // tn),
        in_specs=[
            pl.BlockSpec((tm, d), lambda i, j: (i, 0)),
            pl.BlockSpec((None, 1, d), lambda i, j: (layer, 0, 0)),
            pl.BlockSpec((tm, tn), lambda i, j: (i, j)),
            *w_specs,
            pl.BlockSpec((None, CONV_WIDTH, tn), lambda i, j: (layer, 0, j)),
            pl.BlockSpec((n_seq, CONV_WIDTH - 1, tn), prev_map),
            pl.BlockSpec((None, tn, d), lambda i, j: (layer, j, 0)),
        ],
        out_specs=[
            pl.BlockSpec((tm, d), lambda i, j: (i, 0)),
            pl.BlockSpec((n_seq, CONV_WIDTH - 1, tn), lambda i, j: (i, 0, j)),
        ],
        out_shape=[
            jax.ShapeDtypeStruct((n, d), F32),
            jax.ShapeDtypeStruct((n // tm * n_seq, CONV_WIDTH - 1, d), F32),
        ],
        scratch_shapes=[pltpu.VMEM((tm, d), BF16), pltpu.VMEM((d // tn, CONV_WIDTH - 1, tn), F32)],
        compiler_params=_params("arbitrary", "arbitrary"),
        name="mix",
    )(x, g, attn, w5, w5, w5, w5, w5, cw, prev, wo)
    return y, tail_rows.reshape(-1, tiles_per_seq, CONV_WIDTH - 1, d)[:, -1]


def _cross_body(x_ref, g_ref, wq_ref, gq_ref, mk_ref, mv_ref, wo_ref, o_ref, *, t_seq):
    tm = x_ref.shape[0]
    gq = gq_ref[...]
    cm = min(t_seq, tm)
    for r0 in range(0, tm, cm):
        s = r0 // cm
        x = x_ref[r0:r0 + cm, :]
        h = _rms(x, g_ref[...]).astype(BF16)
        q = _dot(h, wq_ref[...])
        heads = []
        for hd in range(MEM_HEADS):
            c0 = hd * MEM_HEAD_DIM
            qh = (_rms(q[:, c0:c0 + MEM_HEAD_DIM], gq) * (MEM_SCALE * LOG2_E)).astype(BF16)
            kh = mk_ref[s, :, c0:c0 + MEM_HEAD_DIM].astype(BF16)
            vh = mv_ref[s, :, c0:c0 + MEM_HEAD_DIM].astype(BF16)
            sc = _dot_nt(qh, kh)
            p = jnp.exp2(sc - jnp.max(sc, axis=-1, keepdims=True))
            l = jnp.sum(p, axis=-1, keepdims=True)
            heads.append((_dot(p.astype(BF16), vh) * (1.0 / l)).astype(BF16))
        o_ref[r0:r0 + cm, :] = x + _dot(jnp.concatenate(heads, axis=1), wo_ref[...])


def _cross(x, g, wq, gq, mem_k, mem_v, wo, layer, t_seq, tm, mem_layer):
    n, d = x.shape
    _, _, n_mem, dm = mem_k.shape
    if tm >= t_seq:
        n_seq = tm // t_seq
        mem_map = lambda i: (mem_layer, i, 0, 0)
    else:
        n_seq, tiles_per_seq = 1, t_seq // tm
        mem_map = lambda i: (mem_layer, i // tiles_per_seq, 0, 0)
    const = dict(pipeline_mode=pl.Buffered(1))
    return pl.pallas_call(
        functools.partial(_cross_body, t_seq=t_seq),
        grid=(n // tm,),
        in_specs=[
            pl.BlockSpec((tm, d), lambda i: (i, 0)),
            pl.BlockSpec((None, 1, d), lambda i: (layer, 0, 0)),
            pl.BlockSpec((None, d, dm), lambda i: (layer, 0, 0), **const),
            pl.BlockSpec((None, 1, MEM_HEAD_DIM), lambda i: (layer, 0, 0)),
            pl.BlockSpec((None, n_seq, n_mem, dm), mem_map),
            pl.BlockSpec((None, n_seq, n_mem, dm), mem_map),
            pl.BlockSpec((None, dm, d), lambda i: (layer, 0, 0), **const),
        ],
        out_specs=pl.BlockSpec((tm, d), lambda i: (i, 0)),
        out_shape=jax.ShapeDtypeStruct((n, d), F32),
        compiler_params=_params("parallel"),
        name="cross",
    )(x, g, wq, gq, mem_k, mem_v, wo)


def _mem_kv_body(m_ref, g_ref, wk_ref, wv_ref, gk_ref, k_ref, v_ref):
    m = _rms(m_ref[...], g_ref[...]).astype(BF16)
    k = _dot(m, wk_ref[...])
    gk = gk_ref[...]
    for hd in range(MEM_HEADS):
        c0 = hd * MEM_HEAD_DIM
        k_ref[:, c0:c0 + MEM_HEAD_DIM] = _rms(k[:, c0:c0 + MEM_HEAD_DIM], gk)
    v_ref[...] = _dot(m, wv_ref[...])


def _mem_kv(mem, g, wk, wv, gk):
    n, d = mem.shape
    n_layers, _, dm = wk.shape
    tm = _tile(n, 512)
    out = jax.ShapeDtypeStruct((n_layers, n, dm), F32)
    return pl.pallas_call(
        _mem_kv_body,
        grid=(n_layers, n // tm),
        in_specs=[
            pl.BlockSpec((tm, d), lambda l, i: (i, 0)),
            pl.BlockSpec((None, 1, d), lambda l, i: (l, 0, 0)),
            pl.BlockSpec((None, d, dm), lambda l, i: (l, 0, 0)),
            pl.BlockSpec((None, d, dm), lambda l, i: (l, 0, 0)),
            pl.BlockSpec((None, 1, MEM_HEAD_DIM), lambda l, i: (l, 0, 0)),
        ],
        out_specs=[
            pl.BlockSpec((None, tm, dm), lambda l, i: (l, i, 0)),
            pl.BlockSpec((None, tm, dm), lambda l, i: (l, i, 0)),
        ],
        out_shape=[out, out],
        compiler_params=_params("parallel", "parallel"),
        name="mem_kv",
    )(mem, g, wk, wv, gk)


def _rope_tables(pos, reps):
    inv = ROPE_BASE ** (-jnp.arange(ROPE_HALF, dtype=F32) / ROPE_HALF)
    ang = pos.astype(F32)[:, None] * inv[None, :]
    cos, sin = jnp.cos(ang), jnp.sin(ang)
    groups = LANE // QK_ROPE
    cos_t = jnp.tile(jnp.concatenate([cos, cos], axis=1), (reps, groups))
    sin_t = jnp.tile(jnp.concatenate([-sin, sin], axis=1), (reps, groups))
    return cos_t, sin_t


def kernel(x_prompt, x_sample, mem_prompt, cache_kv_latent, cache_k_rope, state_conv, cache_mem_k, cache_mem_v, ffn1_norm, ffn1_w_gate, ffn1_w_up, ffn1_w_down, mix_norm, w_in, q_a_norm, w_uq, q_nope_norm, q_rope_norm, kv_a_norm, w_ukv, k_nope_norm, k_rope_norm, conv_w, w_o, cross_norm, mem_norm, w_cq, w_ck, w_cv, cq_norm, ck_norm, w_co, ffn2_norm, ffn2_w_gate, ffn2_w_up, ffn2_w_down):
    bp, tp, d = x_prompt.shape
    bs, ts, _ = x_sample.shape
    n_layers = w_in.shape[0]
    past = cache_kv_latent.shape[2]
    n_mem = mem_prompt.shape[1]
    n_heads = w_uq.shape[-1] // QK_DIM
    dm = MEM_HEADS * MEM_HEAD_DIM

    row = lambda a: a[:, None, :]
    w1g, w1u, w1d = ffn1_w_gate.astype(BF16), ffn1_w_up.astype(BF16), ffn1_w_down.astype(BF16)
    w2g, w2u, w2d = ffn2_w_gate.astype(BF16), ffn2_w_up.astype(BF16), ffn2_w_down.astype(BF16)
    w_small = jnp.pad(w_in[:, :, :N_SMALL], ((0, 0), (0, 0), (0, SMALL_W - N_SMALL))).astype(BF16)
    w5 = w_in[:, :, N_SMALL:].astype(BF16)
    uq = w_uq.reshape(n_layers, Q_LORA, n_heads, QK_DIM)
    w_uq_p = jnp.concatenate([uq[..., :QK_NOPE].reshape(n_layers, Q_LORA, n_heads * QK_NOPE),
                              uq[..., QK_NOPE:].reshape(n_layers, Q_LORA, n_heads * QK_ROPE)], axis=-1).astype(BF16)
    ukv = w_ukv.reshape(n_layers, KV_LORA, n_heads, QK_NOPE + V_HEAD)
    w_uk_t = jnp.transpose(ukv[..., :QK_NOPE], (0, 2, 3, 1)).reshape(n_layers, n_heads * QK_NOPE, KV_LORA).astype(BF16)
    w_uv = ukv[..., QK_NOPE:].reshape(n_layers, KV_LORA, n_heads * V_HEAD).astype(BF16)
    w_o_b = w_o.astype(BF16)
    w_cq_b, w_ck_b, w_cv_b, w_co_b = w_cq.astype(BF16), w_ck.astype(BF16), w_cv.astype(BF16), w_co.astype(BF16)
    g_qr2 = row(jnp.tile(q_rope_norm, (1, LANE // QK_ROPE)))
    g_kr2 = row(jnp.tile(k_rope_norm, (1, LANE // QK_ROPE)))
    g_kn_col = k_nope_norm[:, :, None]

    tm_p = _tile(tp, 512)
    tm_s = bs * ts
    cos_p, sin_p = _rope_tables(jnp.arange(tp, dtype=jnp.int32), 1)
    cos_s, sin_s = _rope_tables(past + jnp.arange(ts, dtype=jnp.int32), bs)

    tk_s = -(-(past + ts) // 512) * 512
    pad_s = tk_s - past - ts
    cache_krt = jnp.swapaxes(cache_k_rope, -1, -2).astype(BF16)

    mem_k_p, mem_v_p = _mem_kv(mem_prompt.reshape(bp * n_mem, d), row(mem_norm), w_ck_b, w_cv_b, row(ck_norm))
    mem_k_p = mem_k_p.reshape(n_layers, bp, n_mem, dm)
    mem_v_p = mem_v_p.reshape(n_layers, bp, n_mem, dm)
    mem_k_s = cache_mem_k.reshape(n_layers, bs, n_mem, dm)
    mem_v_s = cache_mem_v.reshape(n_layers, bs, n_mem, dm)
    zero_conv = jnp.zeros((bp, CONV_WIDTH - 1, d), F32)

    yp = x_prompt.reshape(bp * tp, d)
    ys = x_sample.reshape(bs * ts, d)
    cv_p, cv_s = [], []
    slabs_p = slabs_s = None
    for l in range(n_layers):
        def mla(x, cos_t, sin_t, tm, slabs):
            return _mla_proj(x, row(mix_norm), w_small, row(q_a_norm), w_uq_p, row(q_nope_norm), g_qr2,
                             row(kv_a_norm), g_kr2, cos_t, sin_t, l, tm, n_layers, slabs)

        yp = _ffn(yp, row(ffn1_norm), w1g, w1u, w1d, l)
        q, lat_all_p, kr_all_p, nkt = mla(yp, cos_p, sin_p, tm_p, slabs_p)
        slabs_p = (lat_all_p, kr_all_p)
        kt, v = _kv_proj(lat_all_p.reshape(n_layers, bp, tp, KV_LORA), l,
                         jnp.transpose(nkt.reshape(QK_ROPE, bp, tp), (1, 0, 2)), w_uk_t, w_uv, g_kn_col, l)
        attn = _attention(q, kt, v, t=tp, past=0, n_valid=tp)
        yp, nc = _mix(yp, row(mix_norm), attn, w5, conv_w, zero_conv, w_o_b, l, tp, tm_p)
        yp = _cross(yp, row(cross_norm), w_cq_b, row(cq_norm), mem_k_p, mem_v_p, w_co_b, l, tp, tm_p, l)
        yp = _ffn(yp, row(ffn2_norm), w2g, w2u, w2d, l)
        cv_p.append(nc)

        ys = _ffn(ys, row(ffn1_norm), w1g, w1u, w1d, l)
        q, lat_all_s, kr_all_s, nkt = mla(ys, cos_s, sin_s, tm_s, slabs_s)
        slabs_s = (lat_all_s, kr_all_s)
        lat_all = jnp.concatenate([cache_kv_latent[l], lat_all_s[l].reshape(bs, ts, KV_LORA),
                                   jnp.zeros((bs, pad_s, KV_LORA), F32)], axis=1)
        krt_all = jnp.concatenate([cache_krt[l], jnp.transpose(nkt.reshape(QK_ROPE, bs, ts), (1, 0, 2)),
                                   jnp.zeros((bs, QK_ROPE, pad_s), BF16)], axis=2)
        kt, v = _kv_proj(lat_all[None], 0, krt_all, w_uk_t, w_uv, g_kn_col, l)
        attn = _attention(q, kt, v, t=ts, past=past, n_valid=past + ts)
        ys, nc = _mix(ys, row(mix_norm), attn, w5, conv_w, state_conv[l], w_o_b, l, ts, tm_s)
        ys = _cross(ys, row(cross_norm), w_cq_b, row(cq_norm), mem_k_s, mem_v_s, w_co_b, l, ts, tm_s, l)
        ys = _ffn(ys, row(ffn2_norm), w2g, w2u, w2d, l)
        cv_s.append(nc)

    mem_shape = (n_layers, bp, n_mem, MEM_HEADS, MEM_HEAD_DIM)
    return (yp.reshape(bp, tp, d), ys.reshape(bs, ts, d),
            slabs_p[0].reshape(n_layers, bp, tp, KV_LORA), slabs_p[1].reshape(n_layers, bp, tp, QK_ROPE),
            jnp.stack(cv_p), mem_k_p.reshape(mem_shape), mem_v_p.reshape(mem_shape),
            slabs_s[0].reshape(n_layers, bs, ts, KV_LORA), slabs_s[1].reshape(n_layers, bs, ts, QK_ROPE),
            jnp.stack(cv_s))
```

```python
import functools
import math

import jax
import jax.numpy as jnp
from jax import lax
from jax.experimental import pallas as pl
from jax.experimental.pallas import tpu as pltpu

F32 = jnp.float32
BF16 = jnp.bfloat16

V_HEAD = 128
QK_NOPE = 128
QK_ROPE = 64
ROPE_HALF = QK_ROPE // 2
QK_DIM = QK_NOPE + QK_ROPE
Q_LORA = 768
KV_LORA = 512
CHUNK = 64
ROPE_BASE = 10000.0
MEM_HEADS = 4
MEM_HEAD_DIM = 128
CONV_WIDTH = 3
EPS = 1e-6
MLA_SCALE = 1.0 / math.sqrt(QK_NOPE + QK_ROPE)
MEM_SCALE = 1.0 / math.sqrt(MEM_HEAD_DIM)
LOG2_E = math.log2(math.e)
Q_SCALE = MLA_SCALE * LOG2_E

LANE = 128
V7X_VMEM_BYTES = 64 * 1024 * 1024
VMEM_LIMIT_BYTES = V7X_VMEM_BYTES * 13 // 16
LARGE_VMEM_LIMIT_BYTES = V7X_VMEM_BYTES * 15 // 16

FFN_TILE = 512
FFN_ROWS = 1024
FFN_ROW_CHUNK = 512
MIX_TILE = 512
ATTN_HEADS_PER_STEP = 2
KV_CHUNK = 256
N_SMALL = Q_LORA + KV_LORA + QK_ROPE
SMALL_W = Q_LORA + KV_LORA + LANE


def _params(*sem, vmem_limit_bytes=VMEM_LIMIT_BYTES):
    return pltpu.CompilerParams(dimension_semantics=sem, vmem_limit_bytes=vmem_limit_bytes)


def _tile(n, target):
    t = min(n, target)
    while n % t:
        t -= 1
    return t


def _rms(x, g):
    ms = jnp.mean(x * x, axis=-1, keepdims=True)
    return x * lax.rsqrt(ms + EPS) * g


def _sigmoid(x):
    return 1.0 / (1.0 + jnp.exp(-x))


def _dot(a, b):
    return jnp.dot(a, b, preferred_element_type=F32)


def _dot_nt(a, b):
    return lax.dot_general(a, b, (((1,), (1,)), ((), ())), preferred_element_type=F32)


def _ffn_body(x_ref, g_ref, wg_ref, wu_ref, wd_ref, o_ref, h_ref):
    tm = h_ref.shape[0]
    cm = _tile(tm, FFN_ROW_CHUNK)

    def down(h):
        gate = _dot(h, wg_ref[...])
        up = _dot(h, wu_ref[...])
        act = (gate * _sigmoid(gate) * up * 0.5).astype(BF16)
        return _dot(act, wd_ref[...])

    @pl.when(pl.program_id(1) == 0)
    def _():
        for r0 in range(0, tm, cm):
            x = x_ref[r0:r0 + cm, :]
            h = _rms(x, g_ref[...]).astype(BF16)
            h_ref[r0:r0 + cm, :] = h
            o_ref[r0:r0 + cm, :] = x + down(h)

    @pl.when(pl.program_id(1) != 0)
    def _():
        for r0 in range(0, tm, cm):
            o_ref[r0:r0 + cm, :] += down(h_ref[r0:r0 + cm, :])


def _ffn(x, g, wg, wu, wd, layer):
    n, d = x.shape
    f = wg.shape[-1]
    tm, tf = _tile(n, FFN_ROWS), _tile(f, FFN_TILE)
    return pl.pallas_call(
        _ffn_body,
        grid=(n // tm, f // tf),
        in_specs=[
            pl.BlockSpec((tm, d), lambda i, j: (i, 0)),
            pl.BlockSpec((None, 1, d), lambda i, j: (layer, 0, 0)),
            pl.BlockSpec((None, d, tf), lambda i, j: (layer, 0, j)),
            pl.BlockSpec((None, d, tf), lambda i, j: (layer, 0, j)),
            pl.BlockSpec((None, tf, d), lambda i, j: (layer, j, 0)),
        ],
        out_specs=pl.BlockSpec((tm, d), lambda i, j: (i, 0)),
        out_shape=jax.ShapeDtypeStruct((n, d), F32),
        scratch_shapes=[pltpu.VMEM((tm, d), BF16)],
        compiler_params=_params("parallel", "arbitrary", vmem_limit_bytes=LARGE_VMEM_LIMIT_BYTES),
        name="ffn",
    )(x, g, wg, wu, wd)


def _mla_proj_body(x_ref, gmix_ref, ws_ref, gqa_ref, wuq_ref, gqn_ref, gqr_ref, gkva_ref, gkr_ref,
                   cos_ref, sin_ref, *rest, n_heads, row_chunk):
    q_ref, lat_ref, kr_ref, krt_ref = rest[-4:]
    tm = x_ref.shape[0]
    cm = _tile(tm, row_chunk)
    lane = lax.broadcasted_iota(jnp.int32, (cm, LANE), 1)
    first_half = (lane & ROPE_HALF) == 0
    low_group = lane < QK_ROPE
    gqn = gqn_ref[...]
    gqr = gqr_ref[...]
    rope_base = n_heads * QK_NOPE

    for r0 in range(0, tm, cm):
        rs = slice(r0, r0 + cm)
        h = _rms(x_ref[rs, :], gmix_ref[...]).astype(BF16)
        u = _dot(h, ws_ref[...])
        c_q = u[:, :Q_LORA]
        c_kv = u[:, Q_LORA:Q_LORA + KV_LORA]
        k_r = u[:, Q_LORA + KV_LORA:]

        lat_ref[rs, :] = _rms(c_kv, gkva_ref[...])

        cos_t = cos_ref[rs, :]
        sin_t = sin_ref[rs, :]

        def rope(y):
            swapped = jnp.where(first_half, pltpu.roll(y, LANE - ROPE_HALF, 1), pltpu.roll(y, ROPE_HALF, 1))
            return y * cos_t + swapped * sin_t

        ms = jnp.sum(k_r * k_r, axis=-1, keepdims=True) * (1.0 / QK_ROPE)
        kr = rope(k_r * lax.rsqrt(ms + EPS) * gkr_ref[...])
        kr_ref[rs, :] = kr[:, :QK_ROPE]
        krt_ref[:, rs] = kr.T[:QK_ROPE, :].astype(BF16)

        cq_n = _rms(c_q, gqa_ref[...]).astype(BF16)
        q = _dot(cq_n, wuq_ref[...])
        for hd in range(n_heads):
            qn = _rms(q[:, hd * QK_NOPE:(hd + 1) * QK_NOPE], gqn) * Q_SCALE
            q_ref[hd, rs, :QK_NOPE] = qn.astype(BF16)
        for pair in range(n_heads // 2):
            col = q[:, rope_base + pair * LANE: rope_base + (pair + 1) * LANE]
            sq = col * col
            lo = jnp.sum(jnp.where(low_group, sq, 0.0), axis=-1, keepdims=True)
            hi = jnp.sum(jnp.where(low_group, 0.0, sq), axis=-1, keepdims=True)
            ms = jnp.where(low_group, lo, hi) * (1.0 / QK_ROPE)
            y = (rope(col * lax.rsqrt(ms + EPS) * gqr) * Q_SCALE).astype(BF16)
            q_ref[2 * pair, rs, QK_NOPE:] = y[:, :QK_ROPE]
            q_ref[2 * pair + 1, rs, QK_NOPE:] = y[:, QK_ROPE:]


def _mla_proj(x, gmix, ws, gqa, wuq, gqn, gqr2, gkva, gkr2, cos_t, sin_t, layer, tm, n_layers, slabs):
    n, d = x.shape
    alias_specs = [] if slabs is None else [pl.BlockSpec(memory_space=pl.ANY)] * 2
    n_in = 11
    n_heads = wuq.shape[-1] // QK_DIM
    n_pos_tiles = cos_t.shape[0] // tm
    const = dict(pipeline_mode=pl.Buffered(1))
    return pl.pallas_call(
        functools.partial(_mla_proj_body, n_heads=n_heads, row_chunk=256),
        grid=(n // tm,),
        in_specs=[
            pl.BlockSpec((tm, d), lambda i: (i, 0)),
            pl.BlockSpec((None, 1, d), lambda i: (layer, 0, 0)),
            pl.BlockSpec((None, d, SMALL_W), lambda i: (layer, 0, 0), **const),
            pl.BlockSpec((None, 1, Q_LORA), lambda i: (layer, 0, 0)),
            pl.BlockSpec((None, Q_LORA, n_heads * QK_DIM), lambda i: (layer, 0, 0), **const),
            pl.BlockSpec((None, 1, QK_NOPE), lambda i: (layer, 0, 0)),
            pl.BlockSpec((None, 1, LANE), lambda i: (layer, 0, 0)),
            pl.BlockSpec((None, 1, KV_LORA), lambda i: (layer, 0, 0)),
            pl.BlockSpec((None, 1, LANE), lambda i: (layer, 0, 0)),
            pl.BlockSpec((tm, LANE), lambda i: (i % n_pos_tiles, 0)),
            pl.BlockSpec((tm, LANE), lambda i: (i % n_pos_tiles, 0)),
            *alias_specs,
        ],
        out_specs=[
            pl.BlockSpec((n_heads, tm, QK_DIM), lambda i: (0, i, 0)),
            pl.BlockSpec((None, tm, KV_LORA), lambda i: (layer, i, 0)),
            pl.BlockSpec((None, tm, QK_ROPE), lambda i: (layer, i, 0)),
            pl.BlockSpec((QK_ROPE, tm), lambda i: (0, i)),
        ],
        out_shape=[
            jax.ShapeDtypeStruct((n_heads, n, QK_DIM), BF16),
            jax.ShapeDtypeStruct((n_layers, n, KV_LORA), F32),
            jax.ShapeDtypeStruct((n_layers, n, QK_ROPE), F32),
            jax.ShapeDtypeStruct((QK_ROPE, n), BF16),
        ],
        input_output_aliases={} if slabs is None else {n_in: 1, n_in + 1: 2},
        compiler_params=_params("parallel"),
        name="mla_proj",
    )(x, gmix, ws, gqa, wuq, gqn, gqr2, gkva, gkr2, cos_t, sin_t, *(slabs or ()))


def _kv_proj_body(lat_ref, krt_ref, wukt_ref, wuv_ref, gkn_ref, kt_ref, v_ref, *, n_heads):
    tm = lat_ref.shape[0]
    cm = _tile(tm, KV_CHUNK)
    gkn = gkn_ref[...]
    for r0 in range(0, tm, cm):
        lat = lat_ref[r0:r0 + cm, :].astype(BF16)
        knt = _dot_nt(wukt_ref[...], lat)
        krt = krt_ref[:, r0:r0 + cm]
        for hd in range(n_heads):
            blk = knt[hd * QK_NOPE:(hd + 1) * QK_NOPE, :]
            ms = jnp.mean(blk * blk, axis=0, keepdims=True)
            kt_ref[hd, :QK_NOPE, r0:r0 + cm] = (blk * lax.rsqrt(ms + EPS) * gkn).astype(BF16)
            kt_ref[hd, QK_NOPE:, r0:r0 + cm] = krt
        v = _dot(lat, wuv_ref[...])
        for hd in range(n_heads):
            v_ref[hd, r0:r0 + cm, :] = v[:, hd * V_HEAD:(hd + 1) * V_HEAD].astype(BF16)


def _kv_proj(lat, lat_layer, krt, wukt, wuv, gkn_col, layer):
    _, b, tk, _ = lat.shape
    n_heads = wuv.shape[-1] // V_HEAD
    tm = _tile(tk, 512)
    const = dict(pipeline_mode=pl.Buffered(1))
    return pl.pallas_call(
        functools.partial(_kv_proj_body, n_heads=n_heads),
        grid=(b, tk // tm),
        in_specs=[
            pl.BlockSpec((None, None, tm, KV_LORA), lambda i, j: (lat_layer, i, j, 0)),
            pl.BlockSpec((None, QK_ROPE, tm), lambda i, j: (i, 0, j)),
            pl.BlockSpec((None, n_heads * QK_NOPE, KV_LORA), lambda i, j: (layer, 0, 0), **const),
            pl.BlockSpec((None, KV_LORA, n_heads * V_HEAD), lambda i, j: (layer, 0, 0), **const),
            pl.BlockSpec((None, QK_NOPE, 1), lambda i, j: (layer, 0, 0)),
        ],
        out_specs=[
            pl.BlockSpec((None, n_heads, QK_DIM, tm), lambda i, j: (i, 0, 0, j)),
            pl.BlockSpec((None, n_heads, tm, V_HEAD), lambda i, j: (i, 0, j, 0)),
        ],
        out_shape=[
            jax.ShapeDtypeStruct((b, n_heads, QK_DIM, tk), BF16),
            jax.ShapeDtypeStruct((b, n_heads, tk, V_HEAD), BF16),
        ],
        compiler_params=_params("parallel", "parallel"),
        name="kv_proj",
    )(lat, krt, wukt, wuv, gkn_col)


def _attn_body(q_ref, kt_ref, v_ref, o_ref, *, t, tq, past, n_valid, key_tile):
    n_h, _, tk = kt_ref.shape
    blocks = []
    for hd in range(n_h):
        for qs in range(0, t, tq):
            c_lo = (past + qs) // CHUNK
            c_hi = (past + qs + tq - 1) // CHUNK
            full_end = min((c_lo + 1) * CHUNK, n_valid) // LANE * LANE
            edge_end = min(-(-min((c_hi + 1) * CHUNK, n_valid) // LANE) * LANE, tk)
            tiles = [(lo, min(lo + key_tile, full_end), False) for lo in range(0, full_end, key_tile)]
            tiles += [(lo, min(lo + key_tile, edge_end), True) for lo in range(full_end, edge_end, key_tile)]
            blocks.append((hd, qs, tiles))

    def scores(hd, qs, tile):
        lo, hi, needs_mask = tile
        s = _dot(q_ref[hd, qs:qs + tq, :], kt_ref[hd, :, lo:hi])
        if needs_mask:
            q_chunk = (lax.broadcasted_iota(jnp.int32, (tq, hi - lo), 0) + (past + qs)) >> 6
            k_pos = lax.broadcasted_iota(jnp.int32, (tq, hi - lo), 1) + lo
            s = jnp.where(jnp.logical_and((k_pos >> 6) <= q_chunk, k_pos < n_valid), s, -jnp.inf)
        return s

    def lane_groups(a):
        return [a[:, c:c + LANE] for c in range(0, a.shape[1], LANE)]

    cur = [scores(blocks[0][0], blocks[0][1], tile) for tile in blocks[0][2]]
    for bi, (hd, qs, tiles) in enumerate(blocks):
        nxt_hd, nxt_qs, nxt_tiles = blocks[bi + 1] if bi + 1 < len(blocks) else (None, None, [])
        m = jnp.max(functools.reduce(jnp.maximum, [g for s in cur for g in lane_groups(s)]),
                    axis=-1, keepdims=True)
        nxt = []
        l_acc = 0.0
        ps = []
        for k, s in enumerate(cur):
            upto = len(nxt_tiles) if k == len(cur) - 1 else (k + 1) * len(nxt_tiles) // len(cur)
            for tile in nxt_tiles[len(nxt):upto]:
                nxt.append(scores(nxt_hd, nxt_qs, tile))
            p = jnp.exp2(s - m)
            l_acc = l_acc + functools.reduce(jnp.add, lane_groups(p))
            ps.append(p.astype(BF16))
        p_all = ps[0] if len(ps) == 1 else jnp.concatenate(ps, axis=1)
        o = _dot(p_all, v_ref[hd, tiles[0][0]:tiles[-1][1], :])
        o_ref[qs:qs + tq, hd * V_HEAD:(hd + 1) * V_HEAD] = o * (1.0 / jnp.sum(l_acc, axis=-1, keepdims=True))
        cur = nxt


def _attention(q, kt, v, *, t, past, n_valid):
    assert CHUNK == 64
    n_heads, n, _ = q.shape
    b, _, _, tk = kt.shape
    tq = _tile(t, 256)
    hps = _tile(n_heads, ATTN_HEADS_PER_STEP)
    return pl.pallas_call(
        functools.partial(_attn_body, t=t, tq=tq, past=past, n_valid=n_valid, key_tile=256),
        grid=(b, n_heads // hps),
        in_specs=[
            pl.BlockSpec((hps, t, QK_DIM), lambda i, j: (j, i, 0)),
            pl.BlockSpec((None, hps, QK_DIM, tk), lambda i, j: (i, j, 0, 0)),
            pl.BlockSpec((None, hps, tk, V_HEAD), lambda i, j: (i, j, 0, 0)),
        ],
        out_specs=pl.BlockSpec((t, hps * V_HEAD), lambda i, j: (i, j)),
        out_shape=jax.ShapeDtypeStruct((n, n_heads * V_HEAD), F32),
        compiler_params=_params("parallel", "parallel"),
        name="attention",
    )(q, kt, v)


def _kv_attn_body(q_ref, lat_ref, krt_ref, wukt_ref, wuv_ref, gkn_ref, o_ref, kt_ref, v_ref, *, n_heads, t, past, n_valid):
    _kv_proj_body(lat_ref, krt_ref, wukt_ref, wuv_ref, gkn_ref, kt_ref, v_ref, n_heads=n_heads)
    _attn_body(q_ref, kt_ref, v_ref, o_ref, t=t, tq=t, past=past, n_valid=n_valid, key_tile=256)


def _kv_attention(q, lat, krt, wukt, wuv, gkn_col, layer, *, t, past, n_valid):
    n_heads, n, _ = q.shape
    b, tk, _ = lat.shape
    const = dict(pipeline_mode=pl.Buffered(1))
    return pl.pallas_call(
        functools.partial(_kv_attn_body, n_heads=n_heads, t=t, past=past, n_valid=n_valid),
        grid=(b,),
        in_specs=[
            pl.BlockSpec((n_heads, t, QK_DIM), lambda i: (0, i, 0)),
            pl.BlockSpec((None, tk, KV_LORA), lambda i: (i, 0, 0)),
            pl.BlockSpec((None, QK_ROPE, tk), lambda i: (i, 0, 0)),
            pl.BlockSpec((None, n_heads * QK_NOPE, KV_LORA), lambda i: (layer, 0, 0), **const),
            pl.BlockSpec((None, KV_LORA, n_heads * V_HEAD), lambda i: (layer, 0, 0), **const),
            pl.BlockSpec((None, QK_NOPE, 1), lambda i: (layer, 0, 0)),
        ],
        out_specs=pl.BlockSpec((t, n_heads * V_HEAD), lambda i: (i, 0)),
        out_shape=jax.ShapeDtypeStruct((n, n_heads * V_HEAD), F32),
        scratch_shapes=[pltpu.VMEM((n_heads, QK_DIM, tk), BF16), pltpu.VMEM((n_heads, tk, V_HEAD), BF16)],
        compiler_params=_params("arbitrary", vmem_limit_bytes=LARGE_VMEM_LIMIT_BYTES),
        name="kv_attention",
    )(q, lat, krt, wukt, wuv, gkn_col)


def _mix_body(x_ref, g_ref, a_ref, wb_ref, wc_ref, wx_ref, wga_ref, wgc_ref, cw_ref, prev_ref, wo_ref,
              o_ref, nc_ref, h_ref, carry_ref, *, t_seq, tiles_per_seq, row_chunk):
    tm, tn = a_ref.shape
    i = pl.program_id(0)
    j = pl.program_id(1)

    cw = cw_ref[...]
    if tiles_per_seq == 1:
        first_befores = [(s * t_seq, prev_ref[s, 0:1, :], prev_ref[s, 1:2, :]) for s in range(tm // t_seq)]
        cm = tm
    else:
        @pl.when(i % tiles_per_seq == 0)
        def _():
            carry_ref[j] = prev_ref[0]

        first_befores = [(0, carry_ref[j, 0:1, :], carry_ref[j, 1:2, :])]
        cm = _tile(tm, row_chunk)

    def run(first_col_tile):
        befores = first_befores
        for r0 in range(0, tm, cm):
            if first_col_tile:
                x = x_ref[r0:r0 + cm, :]
                h = _rms(x, g_ref[...]).astype(BF16)
                h_ref[r0:r0 + cm, :] = h
            else:
                h = h_ref[r0:r0 + cm, :]
            z = _dot(h, wc_ref[...]) * _dot(h, wx_ref[...])
            rows = lax.broadcasted_iota(jnp.int32, (cm, tn), 0)
            z1 = pltpu.roll(z, 1, 0)
            z2 = pltpu.roll(z, 2, 0)
            for row, p0, p1 in befores:
                z1 = jnp.where(rows == row, p1, z1)
                z2 = jnp.where(rows == row, p0, jnp.where(rows == row + 1, p1, z2))
            befores = [(0, z[cm - 2:cm - 1, :], z[cm - 1:, :])]
            y = cw[0:1, :] * z2 + cw[1:2, :] * z1 + cw[2:3, :] * z
            conv_out = _dot(h, wb_ref[...]) * y
            mixed = (_sigmoid(_dot(h, wga_ref[...])) * a_ref[r0:r0 + cm, :]
                     + _sigmoid(_dot(h, wgc_ref[...])) * conv_out)
            out = _dot(mixed.astype(BF16), wo_ref[...])
            if first_col_tile:
                o_ref[r0:r0 + cm, :] = x + out
            else:
                o_ref[r0:r0 + cm, :] += out
            if tiles_per_seq == 1:
                for s in range(tm // t_seq):
                    nc_ref[s] = z[(s + 1) * t_seq - 2:(s + 1) * t_seq, :]
            elif r0 + cm == tm:
                carry_ref[j] = z[cm - 2:, :]
                nc_ref[0] = z[cm - 2:, :]

    pl.when(j == 0)(functools.partial(run, True))
    pl.when(j != 0)(functools.partial(run, False))


def _mix(x, g, attn, w5, cw, prev, wo, layer, t_seq, tm):
    n, d = x.shape
    tn = _tile(d, MIX_TILE)
    assert CONV_WIDTH == 3 and t_seq >= CONV_WIDTH - 1
    if tm >= t_seq:
        assert tm % t_seq == 0
        n_seq, tiles_per_seq = tm // t_seq, 1
        prev_map = lambda i, j: (i, 0, j)
    else:
        assert t_seq % tm == 0
        n_seq, tiles_per_seq = 1, t_seq // tm
        prev_map = lambda i, j: (i // tiles_per_seq, 0, j)
    w_specs = [pl.BlockSpec((None, d, tn), lambda i, j, c0=k * d // tn: (layer, 0, c0 + j)) for k in range(5)]
    y, tail_rows = pl.pallas_call(
        functools.partial(_mix_body, t_seq=t_seq, tiles_per_seq=tiles_per_seq, row_chunk=256),
        grid=(n // tm, d // tn),
        in_specs=[
            pl.BlockSpec((tm, d), lambda i, j: (i, 0)),
            pl.BlockSpec((None, 1, d), lambda i, j: (layer, 0, 0)),
            pl.BlockSpec((tm, tn), lambda i, j: (i, j)),
            *w_specs,
            pl.BlockSpec((None, CONV_WIDTH, tn), lambda i, j: (layer, 0, j)),
            pl.BlockSpec((n_seq, CONV_WIDTH - 1, tn), prev_map),
            pl.BlockSpec((None, tn, d), lambda i, j: (layer, j, 0)),
        ],
        out_specs=[
            pl.BlockSpec((tm, d), lambda i, j: (i, 0)),
            pl.BlockSpec((n_seq, CONV_WIDTH - 1, tn), lambda i, j: (i, 0, j)),
        ],
        out_shape=[
            jax.ShapeDtypeStruct((n, d), F32),
            jax.ShapeDtypeStruct((n // tm * n_seq, CONV_WIDTH - 1, d), F32),
        ],
        scratch_shapes=[pltpu.VMEM((tm, d), BF16), pltpu.VMEM((d // tn, CONV_WIDTH - 1, tn), F32)],
        compiler_params=_params("arbitrary", "arbitrary"),
        name="mix",
    )(x, g, attn, w5, w5, w5, w5, w5, cw, prev, wo)
    return y, tail_rows.reshape(-1, tiles_per_seq, CONV_WIDTH - 1, d)[:, -1]


def _cross_body(x_ref, g_ref, wq_ref, gq_ref, mk_ref, mv_ref, wo_ref, o_ref, *, t_seq):
    tm = x_ref.shape[0]
    gq = gq_ref[...]
    cm = min(t_seq, tm)
    for r0 in range(0, tm, cm):
        s = r0 // cm
        x = x_ref[r0:r0 + cm, :]
        h = _rms(x, g_ref[...]).astype(BF16)
        q = _dot(h, wq_ref[...])
        heads = []
        for hd in range(MEM_HEADS):
            c0 = hd * MEM_HEAD_DIM
            qh = (_rms(q[:, c0:c0 + MEM_HEAD_DIM], gq) * (MEM_SCALE * LOG2_E)).astype(BF16)
            kh = mk_ref[s, :, c0:c0 + MEM_HEAD_DIM].astype(BF16)
            vh = mv_ref[s, :, c0:c0 + MEM_HEAD_DIM].astype(BF16)
            sc = _dot_nt(qh, kh)
            p = jnp.exp2(sc - jnp.max(sc, axis=-1, keepdims=True))
            l = jnp.sum(p, axis=-1, keepdims=True)
            heads.append((_dot(p.astype(BF16), vh) * (1.0 / l)).astype(BF16))
        o_ref[r0:r0 + cm, :] = x + _dot(jnp.concatenate(heads, axis=1), wo_ref[...])


def _cross(x, g, wq, gq, mem_k, mem_v, wo, layer, t_seq, tm, mem_layer):
    n, d = x.shape
    _, _, n_mem, dm = mem_k.shape
    if tm >= t_seq:
        n_seq = tm // t_seq
        mem_map = lambda i: (mem_layer, i, 0, 0)
    else:
        n_seq, tiles_per_seq = 1, t_seq // tm
        mem_map = lambda i: (mem_layer, i // tiles_per_seq, 0, 0)
    const = dict(pipeline_mode=pl.Buffered(1))
    return pl.pallas_call(
        functools.partial(_cross_body, t_seq=t_seq),
        grid=(n // tm,),
        in_specs=[
            pl.BlockSpec((tm, d), lambda i: (i, 0)),
            pl.BlockSpec((None, 1, d), lambda i: (layer, 0, 0)),
            pl.BlockSpec((None, d, dm), lambda i: (layer, 0, 0), **const),
            pl.BlockSpec((None, 1, MEM_HEAD_DIM), lambda i: (layer, 0, 0)),
            pl.BlockSpec((None, n_seq, n_mem, dm), mem_map),
            pl.BlockSpec((None, n_seq, n_mem, dm), mem_map),
            pl.BlockSpec((None, dm, d), lambda i: (layer, 0, 0), **const),
        ],
        out_specs=pl.BlockSpec((tm, d), lambda i: (i, 0)),
        out_shape=jax.ShapeDtypeStruct((n, d), F32),
        compiler_params=_params("parallel"),
        name="cross",
    )(x, g, wq, gq, mem_k, mem_v, wo)


def _mem_kv_body(m_ref, g_ref, wk_ref, wv_ref, gk_ref, k_ref, v_ref):
    m = _rms(m_ref[...], g_ref[...]).astype(BF16)
    k = _dot(m, wk_ref[...])
    gk = gk_ref[...]
    for hd in range(MEM_HEADS):
        c0 = hd * MEM_HEAD_DIM
        k_ref[:, c0:c0 + MEM_HEAD_DIM] = _rms(k[:, c0:c0 + MEM_HEAD_DIM], gk)
    v_ref[...] = _dot(m, wv_ref[...])


def _mem_kv(mem, g, wk, wv, gk):
    n, d = mem.shape
    n_layers, _, dm = wk.shape
    tm = _tile(n, 512)
    out = jax.ShapeDtypeStruct((n_layers, n, dm), F32)
    return pl.pallas_call(
        _mem_kv_body,
        grid=(n_layers, n // tm),
        in_specs=[
            pl.BlockSpec((tm, d), lambda l, i: (i, 0)),
            pl.BlockSpec((None, 1, d), lambda l, i: (l, 0, 0)),
            pl.BlockSpec((None, d, dm), lambda l, i: (l, 0, 0)),
            pl.BlockSpec((None, d, dm), lambda l, i: (l, 0, 0)),
            pl.BlockSpec((None, 1, MEM_HEAD_DIM), lambda l, i: (l, 0, 0)),
        ],
        out_specs=[
            pl.BlockSpec((None, tm, dm), lambda l, i: (l, i, 0)),
            pl.BlockSpec((None, tm, dm), lambda l, i: (l, i, 0)),
        ],
        out_shape=[out, out],
        compiler_params=_params("parallel", "parallel"),
        name="mem_kv",
    )(mem, g, wk, wv, gk)


def _rope_tables(pos, reps):
    inv = ROPE_BASE ** (-jnp.arange(ROPE_HALF, dtype=F32) / ROPE_HALF)
    ang = pos.astype(F32)[:, None] * inv[None, :]
    cos, sin = jnp.cos(ang), jnp.sin(ang)
    groups = LANE // QK_ROPE
    cos_t = jnp.tile(jnp.concatenate([cos, cos], axis=1), (reps, groups))
    sin_t = jnp.tile(jnp.concatenate([-sin, sin], axis=1), (reps, groups))
    return cos_t, sin_t


def kernel(x_prompt, x_sample, mem_prompt, cache_kv_latent, cache_k_rope, state_conv, cache_mem_k, cache_mem_v, ffn1_norm, ffn1_w_gate, ffn1_w_up, ffn1_w_down, mix_norm, w_in, q_a_norm, w_uq, q_nope_norm, q_rope_norm, kv_a_norm, w_ukv, k_nope_norm, k_rope_norm, conv_w, w_o, cross_norm, mem_norm, w_cq, w_ck, w_cv, cq_norm, ck_norm, w_co, ffn2_norm, ffn2_w_gate, ffn2_w_up, ffn2_w_down):
    bp, tp, d = x_prompt.shape
    bs, ts, _ = x_sample.shape
    n_layers = w_in.shape[0]
    past = cache_kv_latent.shape[2]
    n_mem = mem_prompt.shape[1]
    n_heads = w_uq.shape[-1] // QK_DIM
    dm = MEM_HEADS * MEM_HEAD_DIM

    row = lambda a: a[:, None, :]
    w1g, w1u, w1d = ffn1_w_gate.astype(BF16), ffn1_w_up.astype(BF16), ffn1_w_down.astype(BF16)
    w2g, w2u, w2d = ffn2_w_gate.astype(BF16), ffn2_w_up.astype(BF16), ffn2_w_down.astype(BF16)
    w_small = jnp.pad(w_in[:, :, :N_SMALL], ((0, 0), (0, 0), (0, SMALL_W - N_SMALL))).astype(BF16)
    w5 = w_in[:, :, N_SMALL:].astype(BF16)
    uq = w_uq.reshape(n_layers, Q_LORA, n_heads, QK_DIM)
    w_uq_p = jnp.concatenate([uq[..., :QK_NOPE].reshape(n_layers, Q_LORA, n_heads * QK_NOPE),
                              uq[..., QK_NOPE:].reshape(n_layers, Q_LORA, n_heads * QK_ROPE)], axis=-1).astype(BF16)
    ukv = w_ukv.reshape(n_layers, KV_LORA, n_heads, QK_NOPE + V_HEAD)
    w_uk_t = jnp.transpose(ukv[..., :QK_NOPE], (0, 2, 3, 1)).reshape(n_layers, n_heads * QK_NOPE, KV_LORA).astype(BF16)
    w_uv = ukv[..., QK_NOPE:].reshape(n_layers, KV_LORA, n_heads * V_HEAD).astype(BF16)
    w_o_b = w_o.astype(BF16)
    w_cq_b, w_ck_b, w_cv_b, w_co_b = w_cq.astype(BF16), w_ck.astype(BF16), w_cv.astype(BF16), w_co.astype(BF16)
    g_qr2 = row(jnp.tile(q_rope_norm, (1, LANE // QK_ROPE)))
    g_kr2 = row(jnp.tile(k_rope_norm, (1, LANE // QK_ROPE)))
    g_kn_col = k_nope_norm[:, :, None]

    tm_p = _tile(tp, 512)
    tm_s = bs * ts
    cos_p, sin_p = _rope_tables(jnp.arange(tp, dtype=jnp.int32), 1)
    cos_s, sin_s = _rope_tables(past + jnp.arange(ts, dtype=jnp.int32), bs)

    tk_s = -(-(past + ts) // KV_CHUNK) * KV_CHUNK
    pad_s = tk_s - past - ts
    cache_krt = jnp.swapaxes(cache_k_rope, -1, -2).astype(BF16)

    mem_k_p, mem_v_p = _mem_kv(mem_prompt.reshape(bp * n_mem, d), row(mem_norm), w_ck_b, w_cv_b, row(ck_norm))
    mem_k_p = mem_k_p.reshape(n_layers, bp, n_mem, dm)
    mem_v_p = mem_v_p.reshape(n_layers, bp, n_mem, dm)
    mem_k_s = cache_mem_k.reshape(n_layers, bs, n_mem, dm)
    mem_v_s = cache_mem_v.reshape(n_layers, bs, n_mem, dm)
    zero_conv = jnp.zeros((bp, CONV_WIDTH - 1, d), F32)

    yp = x_prompt.reshape(bp * tp, d)
    ys = x_sample.reshape(bs * ts, d)
    cv_p, cv_s = [], []
    slabs_p = slabs_s = None
    for l in range(n_layers):
        def mla(x, cos_t, sin_t, tm, slabs):
            return _mla_proj(x, row(mix_norm), w_small, row(q_a_norm), w_uq_p, row(q_nope_norm), g_qr2,
                             row(kv_a_norm), g_kr2, cos_t, sin_t, l, tm, n_layers, slabs)

        yp = _ffn(yp, row(ffn1_norm), w1g, w1u, w1d, l)
        q, lat_all_p, kr_all_p, nkt = mla(yp, cos_p, sin_p, tm_p, slabs_p)
        slabs_p = (lat_all_p, kr_all_p)
        kt, v = _kv_proj(lat_all_p.reshape(n_layers, bp, tp, KV_LORA), l,
                         jnp.transpose(nkt.reshape(QK_ROPE, bp, tp), (1, 0, 2)), w_uk_t, w_uv, g_kn_col, l)
        attn = _attention(q, kt, v, t=tp, past=0, n_valid=tp)
        yp, nc = _mix(yp, row(mix_norm), attn, w5, conv_w, zero_conv, w_o_b, l, tp, tm_p)
        yp = _cross(yp, row(cross_norm), w_cq_b, row(cq_norm), mem_k_p, mem_v_p, w_co_b, l, tp, tm_p, l)
        yp = _ffn(yp, row(ffn2_norm), w2g, w2u, w2d, l)
        cv_p.append(nc)

        ys = _ffn(ys, row(ffn1_norm), w1g, w1u, w1d, l)
        q, lat_all_s, kr_all_s, nkt = mla(ys, cos_s, sin_s, tm_s, slabs_s)
        slabs_s = (lat_all_s, kr_all_s)
        lat_all = jnp.concatenate([cache_kv_latent[l], lat_all_s[l].reshape(bs, ts, KV_LORA),
                                   jnp.zeros((bs, pad_s, KV_LORA), F32)], axis=1)
        krt_all = jnp.concatenate([cache_krt[l], jnp.transpose(nkt.reshape(QK_ROPE, bs, ts), (1, 0, 2)),
                                   jnp.zeros((bs, QK_ROPE, pad_s), BF16)], axis=2)
        attn = _kv_attention(q, lat_all, krt_all, w_uk_t, w_uv, g_kn_col, l, t=ts, past=past, n_valid=past + ts)
        ys, nc = _mix(ys, row(mix_norm), attn, w5, conv_w, state_conv[l], w_o_b, l, ts, tm_s)
        ys = _cross(ys, row(cross_norm), w_cq_b, row(cq_norm), mem_k_s, mem_v_s, w_co_b, l, ts, tm_s, l)
        ys = _ffn(ys, row(ffn2_norm), w2g, w2u, w2d, l)
        cv_s.append(nc)

    mem_shape = (n_layers, bp, n_mem, MEM_HEADS, MEM_HEAD_DIM)
    return (yp.reshape(bp, tp, d), ys.reshape(bs, ts, d),
            slabs_p[0].reshape(n_layers, bp, tp, KV_LORA), slabs_p[1].reshape(n_layers, bp, tp, QK_ROPE),
            jnp.stack(cv_p), mem_k_p.reshape(mem_shape), mem_v_p.reshape(mem_shape),
            slabs_s[0].reshape(n_layers, bs, ts, KV_LORA), slabs_s[1].reshape(n_layers, bs, ts, QK_ROPE),
            jnp.stack(cv_s))
```

```python
import functools
import math

import jax
import jax.numpy as jnp
from jax import lax
from jax.experimental import pallas as pl
from jax.experimental.pallas import tpu as pltpu

F32 = jnp.float32
BF16 = jnp.bfloat16

V_HEAD = 128
QK_NOPE = 128
QK_ROPE = 64
ROPE_HALF = QK_ROPE // 2
QK_DIM = QK_NOPE + QK_ROPE
Q_LORA = 768
KV_LORA = 512
CHUNK = 64
ROPE_BASE = 10000.0
MEM_HEADS = 4
MEM_HEAD_DIM = 128
CONV_WIDTH = 3
EPS = 1e-6
MLA_SCALE = 1.0 / math.sqrt(QK_NOPE + QK_ROPE)
MEM_SCALE = 1.0 / math.sqrt(MEM_HEAD_DIM)
LOG2_E = math.log2(math.e)
Q_SCALE = MLA_SCALE * LOG2_E

LANE = 128
V7X_VMEM_BYTES = 64 * 1024 * 1024
VMEM_LIMIT_BYTES = V7X_VMEM_BYTES * 13 // 16
LARGE_VMEM_LIMIT_BYTES = V7X_VMEM_BYTES * 15 // 16

FFN_TILE = 512
FFN_TILE_FEW_ROWS = 1408
FFN_ROWS = 1024
FFN_ROW_CHUNK = 512
MIX_TILE = 512
ATTN_HEADS_PER_STEP = 2
KV_CHUNK = 256
N_SMALL = Q_LORA + KV_LORA + QK_ROPE
SMALL_W = Q_LORA + KV_LORA + LANE


def _params(*sem, vmem_limit_bytes=VMEM_LIMIT_BYTES):
    return pltpu.CompilerParams(dimension_semantics=sem, vmem_limit_bytes=vmem_limit_bytes)


def _tile(n, target):
    t = min(n, target)
    while n % t:
        t -= 1
    return t


def _rms(x, g):
    ms = jnp.mean(x * x, axis=-1, keepdims=True)
    return x * lax.rsqrt(ms + EPS) * g


def _sigmoid(x):
    return 1.0 / (1.0 + jnp.exp(-x))


def _dot(a, b):
    return jnp.dot(a, b, preferred_element_type=F32)


def _dot_nt(a, b):
    return lax.dot_general(a, b, (((1,), (1,)), ((), ())), preferred_element_type=F32)


def _ffn_body(x_ref, g_ref, wg_ref, wu_ref, wd_ref, o_ref, h_ref):
    tm = h_ref.shape[0]
    cm = _tile(tm, FFN_ROW_CHUNK)

    def down(h):
        gate = _dot(h, wg_ref[...])
        up = _dot(h, wu_ref[...])
        act = (gate * _sigmoid(gate) * up * 0.5).astype(BF16)
        return _dot(act, wd_ref[...])

    @pl.when(pl.program_id(1) == 0)
    def _():
        for r0 in range(0, tm, cm):
            x = x_ref[r0:r0 + cm, :]
            h = _rms(x, g_ref[...]).astype(BF16)
            h_ref[r0:r0 + cm, :] = h
            o_ref[r0:r0 + cm, :] = x + down(h)

    @pl.when(pl.program_id(1) != 0)
    def _():
        for r0 in range(0, tm, cm):
            o_ref[r0:r0 + cm, :] += down(h_ref[r0:r0 + cm, :])


def _ffn(x, g, wg, wu, wd, layer):
    n, d = x.shape
    f = wg.shape[-1]
    tm, tf = _tile(n, FFN_ROWS), _tile(f, FFN_TILE if n > FFN_ROW_CHUNK else FFN_TILE_FEW_ROWS)
    return pl.pallas_call(
        _ffn_body,
        grid=(n // tm, f // tf),
        in_specs=[
            pl.BlockSpec((tm, d), lambda i, j: (i, 0)),
            pl.BlockSpec((None, 1, d), lambda i, j: (layer, 0, 0)),
            pl.BlockSpec((None, d, tf), lambda i, j: (layer, 0, j)),
            pl.BlockSpec((None, d, tf), lambda i, j: (layer, 0, j)),
            pl.BlockSpec((None, tf, d), lambda i, j: (layer, j, 0)),
        ],
        out_specs=pl.BlockSpec((tm, d), lambda i, j: (i, 0)),
        out_shape=jax.ShapeDtypeStruct((n, d), F32),
        scratch_shapes=[pltpu.VMEM((tm, d), BF16)],
        compiler_params=_params("parallel", "arbitrary", vmem_limit_bytes=LARGE_VMEM_LIMIT_BYTES),
        name="ffn",
    )(x, g, wg, wu, wd)


def _mla_proj_body(x_ref, gmix_ref, ws_ref, gqa_ref, wuq_ref, gqn_ref, gqr_ref, gkva_ref, gkr_ref,
                   cos_ref, sin_ref, *rest, n_heads, row_chunk):
    q_ref, lat_ref, kr_ref, krt_ref = rest[-4:]
    tm = x_ref.shape[0]
    cm = _tile(tm, row_chunk)
    lane = lax.broadcasted_iota(jnp.int32, (cm, LANE), 1)
    first_half = (lane & ROPE_HALF) == 0
    low_group = lane < QK_ROPE
    gqn = gqn_ref[...]
    gqr = gqr_ref[...]
    rope_base = n_heads * QK_NOPE

    for r0 in range(0, tm, cm):
        rs = slice(r0, r0 + cm)
        h = _rms(x_ref[rs, :], gmix_ref[...]).astype(BF16)
        u = _dot(h, ws_ref[...])
        c_q = u[:, :Q_LORA]
        c_kv = u[:, Q_LORA:Q_LORA + KV_LORA]
        k_r = u[:, Q_LORA + KV_LORA:]

        lat_ref[rs, :] = _rms(c_kv, gkva_ref[...])

        cos_t = cos_ref[rs, :]
        sin_t = sin_ref[rs, :]

        def rope(y):
            swapped = jnp.where(first_half, pltpu.roll(y, LANE - ROPE_HALF, 1), pltpu.roll(y, ROPE_HALF, 1))
            return y * cos_t + swapped * sin_t

        ms = jnp.sum(k_r * k_r, axis=-1, keepdims=True) * (1.0 / QK_ROPE)
        kr = rope(k_r * lax.rsqrt(ms + EPS) * gkr_ref[...])
        kr_ref[rs, :] = kr[:, :QK_ROPE]
        krt_ref[:, rs] = kr.T[:QK_ROPE, :].astype(BF16)

        cq_n = _rms(c_q, gqa_ref[...]).astype(BF16)
        q = _dot(cq_n, wuq_ref[...])
        for hd in range(n_heads):
            qn = _rms(q[:, hd * QK_NOPE:(hd + 1) * QK_NOPE], gqn) * Q_SCALE
            q_ref[hd, rs, :QK_NOPE] = qn.astype(BF16)
        for pair in range(n_heads // 2):
            col = q[:, rope_base + pair * LANE: rope_base + (pair + 1) * LANE]
            sq = col * col
            lo = jnp.sum(jnp.where(low_group, sq, 0.0), axis=-1, keepdims=True)
            hi = jnp.sum(jnp.where(low_group, 0.0, sq), axis=-1, keepdims=True)
            ms = jnp.where(low_group, lo, hi) * (1.0 / QK_ROPE)
            y = (rope(col * lax.rsqrt(ms + EPS) * gqr) * Q_SCALE).astype(BF16)
            q_ref[2 * pair, rs, QK_NOPE:] = y[:, :QK_ROPE]
            q_ref[2 * pair + 1, rs, QK_NOPE:] = y[:, QK_ROPE:]


def _mla_proj(x, gmix, ws, gqa, wuq, gqn, gqr2, gkva, gkr2, cos_t, sin_t, layer, tm, n_layers, slabs):
    n, d = x.shape
    alias_specs = [] if slabs is None else [pl.BlockSpec(memory_space=pl.ANY)] * 2
    n_in = 11
    n_heads = wuq.shape[-1] // QK_DIM
    n_pos_tiles = cos_t.shape[0] // tm
    const = dict(pipeline_mode=pl.Buffered(1))
    return pl.pallas_call(
        functools.partial(_mla_proj_body, n_heads=n_heads, row_chunk=256),
        grid=(n // tm,),
        in_specs=[
            pl.BlockSpec((tm, d), lambda i: (i, 0)),
            pl.BlockSpec((None, 1, d), lambda i: (layer, 0, 0)),
            pl.BlockSpec((None, d, SMALL_W), lambda i: (layer, 0, 0), **const),
            pl.BlockSpec((None, 1, Q_LORA), lambda i: (layer, 0, 0)),
            pl.BlockSpec((None, Q_LORA, n_heads * QK_DIM), lambda i: (layer, 0, 0), **const),
            pl.BlockSpec((None, 1, QK_NOPE), lambda i: (layer, 0, 0)),
            pl.BlockSpec((None, 1, LANE), lambda i: (layer, 0, 0)),
            pl.BlockSpec((None, 1, KV_LORA), lambda i: (layer, 0, 0)),
            pl.BlockSpec((None, 1, LANE), lambda i: (layer, 0, 0)),
            pl.BlockSpec((tm, LANE), lambda i: (i % n_pos_tiles, 0)),
            pl.BlockSpec((tm, LANE), lambda i: (i % n_pos_tiles, 0)),
            *alias_specs,
        ],
        out_specs=[
            pl.BlockSpec((n_heads, tm, QK_DIM), lambda i: (0, i, 0)),
            pl.BlockSpec((None, tm, KV_LORA), lambda i: (layer, i, 0)),
            pl.BlockSpec((None, tm, QK_ROPE), lambda i: (layer, i, 0)),
            pl.BlockSpec((QK_ROPE, tm), lambda i: (0, i)),
        ],
        out_shape=[
            jax.ShapeDtypeStruct((n_heads, n, QK_DIM), BF16),
            jax.ShapeDtypeStruct((n_layers, n, KV_LORA), F32),
            jax.ShapeDtypeStruct((n_layers, n, QK_ROPE), F32),
            jax.ShapeDtypeStruct((QK_ROPE, n), BF16),
        ],
        input_output_aliases={} if slabs is None else {n_in: 1, n_in + 1: 2},
        compiler_params=_params("parallel"),
        name="mla_proj",
    )(x, gmix, ws, gqa, wuq, gqn, gqr2, gkva, gkr2, cos_t, sin_t, *(slabs or ()))


def _kv_proj_body(lat_ref, krt_ref, wukt_ref, wuv_ref, gkn_ref, kt_ref, v_ref, *, n_heads):
    tm = lat_ref.shape[0]
    cm = _tile(tm, KV_CHUNK)
    gkn = gkn_ref[...]
    for r0 in range(0, tm, cm):
        lat = lat_ref[r0:r0 + cm, :].astype(BF16)
        knt = _dot_nt(wukt_ref[...], lat)
        krt = krt_ref[:, r0:r0 + cm]
        for hd in range(n_heads):
            blk = knt[hd * QK_NOPE:(hd + 1) * QK_NOPE, :]
            ms = jnp.mean(blk * blk, axis=0, keepdims=True)
            kt_ref[hd, :QK_NOPE, r0:r0 + cm] = (blk * lax.rsqrt(ms + EPS) * gkn).astype(BF16)
            kt_ref[hd, QK_NOPE:, r0:r0 + cm] = krt
        v = _dot(lat, wuv_ref[...])
        for hd in range(n_heads):
            v_ref[hd, r0:r0 + cm, :] = v[:, hd * V_HEAD:(hd + 1) * V_HEAD].astype(BF16)


def _kv_proj(lat, lat_layer, krt, wukt, wuv, gkn_col, layer):
    _, b, tk, _ = lat.shape
    n_heads = wuv.shape[-1] // V_HEAD
    tm = _tile(tk, 512)
    const = dict(pipeline_mode=pl.Buffered(1))
    return pl.pallas_call(
        functools.partial(_kv_proj_body, n_heads=n_heads),
        grid=(b, tk // tm),
        in_specs=[
            pl.BlockSpec((None, None, tm, KV_LORA), lambda i, j: (lat_layer, i, j, 0)),
            pl.BlockSpec((None, QK_ROPE, tm), lambda i, j: (i, 0, j)),
            pl.BlockSpec((None, n_heads * QK_NOPE, KV_LORA), lambda i, j: (layer, 0, 0), **const),
            pl.BlockSpec((None, KV_LORA, n_heads * V_HEAD), lambda i, j: (layer, 0, 0), **const),
            pl.BlockSpec((None, QK_NOPE, 1), lambda i, j: (layer, 0, 0)),
        ],
        out_specs=[
            pl.BlockSpec((None, n_heads, QK_DIM, tm), lambda i, j: (i, 0, 0, j)),
            pl.BlockSpec((None, n_heads, tm, V_HEAD), lambda i, j: (i, 0, j, 0)),
        ],
        out_shape=[
            jax.ShapeDtypeStruct((b, n_heads, QK_DIM, tk), BF16),
            jax.ShapeDtypeStruct((b, n_heads, tk, V_HEAD), BF16),
        ],
        compiler_params=_params("parallel", "parallel"),
        name="kv_proj",
    )(lat, krt, wukt, wuv, gkn_col)


def _attn_body(q_ref, kt_ref, v_ref, o_ref, *, t, tq, past, n_valid, key_tile):
    n_h, _, tk = kt_ref.shape
    blocks = []
    for hd in range(n_h):
        for qs in range(0, t, tq):
            c_lo = (past + qs) // CHUNK
            c_hi = (past + qs + tq - 1) // CHUNK
            full_end = min((c_lo + 1) * CHUNK, n_valid) // LANE * LANE
            edge_end = min(-(-min((c_hi + 1) * CHUNK, n_valid) // LANE) * LANE, tk)
            tiles = [(lo, min(lo + key_tile, full_end), False) for lo in range(0, full_end, key_tile)]
            tiles += [(lo, min(lo + key_tile, edge_end), True) for lo in range(full_end, edge_end, key_tile)]
            blocks.append((hd, qs, tiles))

    def scores(hd, qs, tile):
        lo, hi, needs_mask = tile
        s = _dot(q_ref[hd, qs:qs + tq, :], kt_ref[hd, :, lo:hi])
        if needs_mask:
            q_chunk = (lax.broadcasted_iota(jnp.int32, (tq, hi - lo), 0) + (past + qs)) >> 6
            k_pos = lax.broadcasted_iota(jnp.int32, (tq, hi - lo), 1) + lo
            s = jnp.where(jnp.logical_and((k_pos >> 6) <= q_chunk, k_pos < n_valid), s, -jnp.inf)
        return s

    def lane_groups(a):
        return [a[:, c:c + LANE] for c in range(0, a.shape[1], LANE)]

    cur = [scores(blocks[0][0], blocks[0][1], tile) for tile in blocks[0][2]]
    for bi, (hd, qs, tiles) in enumerate(blocks):
        nxt_hd, nxt_qs, nxt_tiles = blocks[bi + 1] if bi + 1 < len(blocks) else (None, None, [])
        m = jnp.max(functools.reduce(jnp.maximum, [g for s in cur for g in lane_groups(s)]),
                    axis=-1, keepdims=True)
        nxt = []
        l_acc = 0.0
        ps = []
        for k, s in enumerate(cur):
            upto = len(nxt_tiles) if k == len(cur) - 1 else (k + 1) * len(nxt_tiles) // len(cur)
            for tile in nxt_tiles[len(nxt):upto]:
                nxt.append(scores(nxt_hd, nxt_qs, tile))
            p = jnp.exp2(s - m)
            l_acc = l_acc + functools.reduce(jnp.add, lane_groups(p))
            ps.append(p.astype(BF16))
        p_all = ps[0] if len(ps) == 1 else jnp.concatenate(ps, axis=1)
        o = _dot(p_all, v_ref[hd, tiles[0][0]:tiles[-1][1], :])
        o_ref[qs:qs + tq, hd * V_HEAD:(hd + 1) * V_HEAD] = o * (1.0 / jnp.sum(l_acc, axis=-1, keepdims=True))
        cur = nxt


def _attention(q, kt, v, *, t, past, n_valid):
    assert CHUNK == 64
    n_heads, n, _ = q.shape
    b, _, _, tk = kt.shape
    tq = _tile(t, 256)
    hps = _tile(n_heads, ATTN_HEADS_PER_STEP)
    return pl.pallas_call(
        functools.partial(_attn_body, t=t, tq=tq, past=past, n_valid=n_valid, key_tile=256),
        grid=(b, n_heads // hps),
        in_specs=[
            pl.BlockSpec((hps, t, QK_DIM), lambda i, j: (j, i, 0)),
            pl.BlockSpec((None, hps, QK_DIM, tk), lambda i, j: (i, j, 0, 0)),
            pl.BlockSpec((None, hps, tk, V_HEAD), lambda i, j: (i, j, 0, 0)),
        ],
        out_specs=pl.BlockSpec((t, hps * V_HEAD), lambda i, j: (i, j)),
        out_shape=jax.ShapeDtypeStruct((n, n_heads * V_HEAD), F32),
        compiler_params=_params("parallel", "parallel"),
        name="attention",
    )(q, kt, v)


def _kv_attn_body(q_ref, lat_ref, krt_ref, wukt_ref, wuv_ref, gkn_ref, o_ref, kt_ref, v_ref, *, n_heads, t, past, n_valid):
    _kv_proj_body(lat_ref, krt_ref, wukt_ref, wuv_ref, gkn_ref, kt_ref, v_ref, n_heads=n_heads)
    _attn_body(q_ref, kt_ref, v_ref, o_ref, t=t, tq=t, past=past, n_valid=n_valid, key_tile=256)


def _kv_attention(q, lat, krt, wukt, wuv, gkn_col, layer, *, t, past, n_valid):
    n_heads, n, _ = q.shape
    _, b, tk, _ = lat.shape
    const = dict(pipeline_mode=pl.Buffered(1))
    return pl.pallas_call(
        functools.partial(_kv_attn_body, n_heads=n_heads, t=t, past=past, n_valid=n_valid),
        grid=(b,),
        in_specs=[
            pl.BlockSpec((n_heads, t, QK_DIM), lambda i: (0, i, 0)),
            pl.BlockSpec((None, None, tk, KV_LORA), lambda i: (layer, i, 0, 0)),
            pl.BlockSpec((None, None, QK_ROPE, tk), lambda i: (layer, i, 0, 0)),
            pl.BlockSpec((None, n_heads * QK_NOPE, KV_LORA), lambda i: (layer, 0, 0), **const),
            pl.BlockSpec((None, KV_LORA, n_heads * V_HEAD), lambda i: (layer, 0, 0), **const),
            pl.BlockSpec((None, QK_NOPE, 1), lambda i: (layer, 0, 0)),
        ],
        out_specs=pl.BlockSpec((t, n_heads * V_HEAD), lambda i: (i, 0)),
        out_shape=jax.ShapeDtypeStruct((n, n_heads * V_HEAD), F32),
        scratch_shapes=[pltpu.VMEM((n_heads, QK_DIM, tk), BF16), pltpu.VMEM((n_heads, tk, V_HEAD), BF16)],
        compiler_params=_params("arbitrary", vmem_limit_bytes=LARGE_VMEM_LIMIT_BYTES),
        name="kv_attention",
    )(q, lat, krt, wukt, wuv, gkn_col)


def _mix_body(x_ref, g_ref, a_ref, wb_ref, wc_ref, wx_ref, wga_ref, wgc_ref, cw_ref, prev_ref, wo_ref,
              o_ref, nc_ref, h_ref, carry_ref, *, t_seq, tiles_per_seq, row_chunk):
    tm, tn = a_ref.shape
    i = pl.program_id(0)
    j = pl.program_id(1)

    cw = cw_ref[...]
    if tiles_per_seq == 1:
        first_befores = [(s * t_seq, prev_ref[s, 0:1, :], prev_ref[s, 1:2, :]) for s in range(tm // t_seq)]
        cm = tm
    else:
        @pl.when(i % tiles_per_seq == 0)
        def _():
            carry_ref[j] = prev_ref[0]

        first_befores = [(0, carry_ref[j, 0:1, :], carry_ref[j, 1:2, :])]
        cm = _tile(tm, row_chunk)

    def run(first_col_tile):
        befores = first_befores
        for r0 in range(0, tm, cm):
            if first_col_tile:
                x = x_ref[r0:r0 + cm, :]
                h = _rms(x, g_ref[...]).astype(BF16)
                h_ref[r0:r0 + cm, :] = h
            else:
                h = h_ref[r0:r0 + cm, :]
            z = _dot(h, wc_ref[...]) * _dot(h, wx_ref[...])
            rows = lax.broadcasted_iota(jnp.int32, (cm, tn), 0)
            z1 = pltpu.roll(z, 1, 0)
            z2 = pltpu.roll(z, 2, 0)
            for row, p0, p1 in befores:
                z1 = jnp.where(rows == row, p1, z1)
                z2 = jnp.where(rows == row, p0, jnp.where(rows == row + 1, p1, z2))
            befores = [(0, z[cm - 2:cm - 1, :], z[cm - 1:, :])]
            y = cw[0:1, :] * z2 + cw[1:2, :] * z1 + cw[2:3, :] * z
            conv_out = _dot(h, wb_ref[...]) * y
            mixed = (_sigmoid(_dot(h, wga_ref[...])) * a_ref[r0:r0 + cm, :]
                     + _sigmoid(_dot(h, wgc_ref[...])) * conv_out)
            out = _dot(mixed.astype(BF16), wo_ref[...])
            if first_col_tile:
                o_ref[r0:r0 + cm, :] = x + out
            else:
                o_ref[r0:r0 + cm, :] += out
            if tiles_per_seq == 1:
                for s in range(tm // t_seq):
                    nc_ref[s] = z[(s + 1) * t_seq - 2:(s + 1) * t_seq, :]
            elif r0 + cm == tm:
                carry_ref[j] = z[cm - 2:, :]
                nc_ref[0] = z[cm - 2:, :]

    pl.when(j == 0)(functools.partial(run, True))
    pl.when(j != 0)(functools.partial(run, False))


def _mix(x, g, attn, w5, cw, prev, wo, layer, t_seq, tm):
    n, d = x.shape
    tn = _tile(d, MIX_TILE)
    assert CONV_WIDTH == 3 and t_seq >= CONV_WIDTH - 1
    if tm >= t_seq:
        assert tm % t_seq == 0
        n_seq, tiles_per_seq = tm // t_seq, 1
        prev_map = lambda i, j: (i, 0, j)
    else:
        assert t_seq % tm == 0
        n_seq, tiles_per_seq = 1, t_seq // tm
        prev_map = lambda i, j: (i // tiles_per_seq, 0, j)
    w_specs = [pl.BlockSpec((None, d, tn), lambda i, j, c0=k * d // tn: (layer, 0, c0 + j)) for k in range(5)]
    y, tail_rows = pl.pallas_call(
        functools.partial(_mix_body, t_seq=t_seq, tiles_per_seq=tiles_per_seq, row_chunk=256),
        grid=(n // tm, d // tn),
        in_specs=[
            pl.BlockSpec((tm, d), lambda i, j: (i, 0)),
            pl.BlockSpec((None, 1, d), lambda i, j: (layer, 0, 0)),
            pl.BlockSpec((tm, tn), lambda i, j: (i, j)),
            *w_specs,
            pl.BlockSpec((None, CONV_WIDTH, tn), lambda i, j: (layer, 0, j)),
            pl.BlockSpec((n_seq, CONV_WIDTH - 1, tn), prev_map),
            pl.BlockSpec((None, tn, d), lambda i, j: (layer, j, 0)),
        ],
        out_specs=[
            pl.BlockSpec((tm, d), lambda i, j: (i, 0)),
            pl.BlockSpec((n_seq, CONV_WIDTH - 1, tn), lambda i, j: (i, 0, j)),
        ],
        out_shape=[
            jax.ShapeDtypeStruct((n, d), F32),
            jax.ShapeDtypeStruct((n // tm * n_seq, CONV_WIDTH - 1, d), F32),
        ],
        scratch_shapes=[pltpu.VMEM((tm, d), BF16), pltpu.VMEM((d // tn, CONV_WIDTH - 1, tn), F32)],
        compiler_params=_params("arbitrary", "arbitrary"),
        name="mix",
    )(x, g, attn, w5, w5, w5, w5, w5, cw, prev, wo)
    return y, tail_rows.reshape(-1, tiles_per_seq, CONV_WIDTH - 1, d)[:, -1]


def _cross_body(x_ref, g_ref, wq_ref, gq_ref, mk_ref, mv_ref, wo_ref, o_ref, *, t_seq):
    tm = x_ref.shape[0]
    gq = gq_ref[...]
    cm = min(t_seq, tm)
    for r0 in range(0, tm, cm):
        s = r0 // cm
        x = x_ref[r0:r0 + cm, :]
        h = _rms(x, g_ref[...]).astype(BF16)
        q = _dot(h, wq_ref[...])
        heads = []
        for hd in range(MEM_HEADS):
            c0 = hd * MEM_HEAD_DIM
            qh = (_rms(q[:, c0:c0 + MEM_HEAD_DIM], gq) * (MEM_SCALE * LOG2_E)).astype(BF16)
            kh = mk_ref[s, :, c0:c0 + MEM_HEAD_DIM].astype(BF16)
            vh = mv_ref[s, :, c0:c0 + MEM_HEAD_DIM].astype(BF16)
            sc = _dot_nt(qh, kh)
            p = jnp.exp2(sc - jnp.max(sc, axis=-1, keepdims=True))
            l = jnp.sum(p, axis=-1, keepdims=True)
            heads.append((_dot(p.astype(BF16), vh) * (1.0 / l)).astype(BF16))
        o_ref[r0:r0 + cm, :] = x + _dot(jnp.concatenate(heads, axis=1), wo_ref[...])


def _cross(x, g, wq, gq, mem_k, mem_v, wo, layer, t_seq, tm, mem_layer):
    n, d = x.shape
    _, _, n_mem, dm = mem_k.shape
    if tm >= t_seq:
        n_seq = tm // t_seq
        mem_map = lambda i: (mem_layer, i, 0, 0)
    else:
        n_seq, tiles_per_seq = 1, t_seq // tm
        mem_map = lambda i: (mem_layer, i // tiles_per_seq, 0, 0)
    const = dict(pipeline_mode=pl.Buffered(1))
    return pl.pallas_call(
        functools.partial(_cross_body, t_seq=t_seq),
        grid=(n // tm,),
        in_specs=[
            pl.BlockSpec((tm, d), lambda i: (i, 0)),
            pl.BlockSpec((None, 1, d), lambda i: (layer, 0, 0)),
            pl.BlockSpec((None, d, dm), lambda i: (layer, 0, 0), **const),
            pl.BlockSpec((None, 1, MEM_HEAD_DIM), lambda i: (layer, 0, 0)),
            pl.BlockSpec((None, n_seq, n_mem, dm), mem_map),
            pl.BlockSpec((None, n_seq, n_mem, dm), mem_map),
            pl.BlockSpec((None, dm, d), lambda i: (layer, 0, 0), **const),
        ],
        out_specs=pl.BlockSpec((tm, d), lambda i: (i, 0)),
        out_shape=jax.ShapeDtypeStruct((n, d), F32),
        compiler_params=_params("parallel"),
        name="cross",
    )(x, g, wq, gq, mem_k, mem_v, wo)


def _mem_kv_body(m_ref, g_ref, wk_ref, wv_ref, gk_ref, k_ref, v_ref):
    m = _rms(m_ref[...], g_ref[...]).astype(BF16)
    k = _dot(m, wk_ref[...])
    gk = gk_ref[...]
    for hd in range(MEM_HEADS):
        c0 = hd * MEM_HEAD_DIM
        k_ref[:, c0:c0 + MEM_HEAD_DIM] = _rms(k[:, c0:c0 + MEM_HEAD_DIM], gk)
    v_ref[...] = _dot(m, wv_ref[...])


def _mem_kv(mem, g, wk, wv, gk):
    n, d = mem.shape
    n_layers, _, dm = wk.shape
    tm = _tile(n, 512)
    out = jax.ShapeDtypeStruct((n_layers, n, dm), F32)
    return pl.pallas_call(
        _mem_kv_body,
        grid=(n_layers, n // tm),
        in_specs=[
            pl.BlockSpec((tm, d), lambda l, i: (i, 0)),
            pl.BlockSpec((None, 1, d), lambda l, i: (l, 0, 0)),
            pl.BlockSpec((None, d, dm), lambda l, i: (l, 0, 0)),
            pl.BlockSpec((None, d, dm), lambda l, i: (l, 0, 0)),
            pl.BlockSpec((None, 1, MEM_HEAD_DIM), lambda l, i: (l, 0, 0)),
        ],
        out_specs=[
            pl.BlockSpec((None, tm, dm), lambda l, i: (l, i, 0)),
            pl.BlockSpec((None, tm, dm), lambda l, i: (l, i, 0)),
        ],
        out_shape=[out, out],
        compiler_params=_params("parallel", "parallel"),
        name="mem_kv",
    )(mem, g, wk, wv, gk)


def _rope_tables(pos, reps):
    inv = ROPE_BASE ** (-jnp.arange(ROPE_HALF, dtype=F32) / ROPE_HALF)
    ang = pos.astype(F32)[:, None] * inv[None, :]
    cos, sin = jnp.cos(ang), jnp.sin(ang)
    groups = LANE // QK_ROPE
    cos_t = jnp.tile(jnp.concatenate([cos, cos], axis=1), (reps, groups))
    sin_t = jnp.tile(jnp.concatenate([-sin, sin], axis=1), (reps, groups))
    return cos_t, sin_t


def kernel(x_prompt, x_sample, mem_prompt, cache_kv_latent, cache_k_rope, state_conv, cache_mem_k, cache_mem_v, ffn1_norm, ffn1_w_gate, ffn1_w_up, ffn1_w_down, mix_norm, w_in, q_a_norm, w_uq, q_nope_norm, q_rope_norm, kv_a_norm, w_ukv, k_nope_norm, k_rope_norm, conv_w, w_o, cross_norm, mem_norm, w_cq, w_ck, w_cv, cq_norm, ck_norm, w_co, ffn2_norm, ffn2_w_gate, ffn2_w_up, ffn2_w_down):
    bp, tp, d = x_prompt.shape
    bs, ts, _ = x_sample.shape
    n_layers = w_in.shape[0]
    past = cache_kv_latent.shape[2]
    n_mem = mem_prompt.shape[1]
    n_heads = w_uq.shape[-1] // QK_DIM
    dm = MEM_HEADS * MEM_HEAD_DIM

    row = lambda a: a[:, None, :]
    w1g, w1u, w1d = ffn1_w_gate.astype(BF16), ffn1_w_up.astype(BF16), ffn1_w_down.astype(BF16)
    w2g, w2u, w2d = ffn2_w_gate.astype(BF16), ffn2_w_up.astype(BF16), ffn2_w_down.astype(BF16)
    w_small = jnp.pad(w_in[:, :, :N_SMALL], ((0, 0), (0, 0), (0, SMALL_W - N_SMALL))).astype(BF16)
    w5 = w_in[:, :, N_SMALL:].astype(BF16)
    uq = w_uq.reshape(n_layers, Q_LORA, n_heads, QK_DIM)
    w_uq_p = jnp.concatenate([uq[..., :QK_NOPE].reshape(n_layers, Q_LORA, n_heads * QK_NOPE),
                              uq[..., QK_NOPE:].reshape(n_layers, Q_LORA, n_heads * QK_ROPE)], axis=-1).astype(BF16)
    ukv = w_ukv.reshape(n_layers, KV_LORA, n_heads, QK_NOPE + V_HEAD)
    w_uk_t = jnp.transpose(ukv[..., :QK_NOPE], (0, 2, 3, 1)).reshape(n_layers, n_heads * QK_NOPE, KV_LORA).astype(BF16)
    w_uv = ukv[..., QK_NOPE:].reshape(n_layers, KV_LORA, n_heads * V_HEAD).astype(BF16)
    w_o_b = w_o.astype(BF16)
    w_cq_b, w_ck_b, w_cv_b, w_co_b = w_cq.astype(BF16), w_ck.astype(BF16), w_cv.astype(BF16), w_co.astype(BF16)
    g_qr2 = row(jnp.tile(q_rope_norm, (1, LANE // QK_ROPE)))
    g_kr2 = row(jnp.tile(k_rope_norm, (1, LANE // QK_ROPE)))
    g_kn_col = k_nope_norm[:, :, None]

    tm_p = _tile(tp, 512)
    tm_s = bs * ts
    cos_p, sin_p = _rope_tables(jnp.arange(tp, dtype=jnp.int32), 1)
    cos_s, sin_s = _rope_tables(past + jnp.arange(ts, dtype=jnp.int32), bs)

    tk_s = -(-(past + ts) // KV_CHUNK) * KV_CHUNK
    lat_keys = jnp.pad(cache_kv_latent, ((0, 0), (0, 0), (0, tk_s - past), (0, 0)))
    krt_keys = jnp.pad(jnp.swapaxes(cache_k_rope, -1, -2).astype(BF16), ((0, 0), (0, 0), (0, 0), (0, tk_s - past)))

    mem_k_p, mem_v_p = _mem_kv(mem_prompt.reshape(bp * n_mem, d), row(mem_norm), w_ck_b, w_cv_b, row(ck_norm))
    mem_k_p = mem_k_p.reshape(n_layers, bp, n_mem, dm)
    mem_v_p = mem_v_p.reshape(n_layers, bp, n_mem, dm)
    mem_k_s = cache_mem_k.reshape(n_layers, bs, n_mem, dm)
    mem_v_s = cache_mem_v.reshape(n_layers, bs, n_mem, dm)
    zero_conv = jnp.zeros((bp, CONV_WIDTH - 1, d), F32)

    yp = x_prompt.reshape(bp * tp, d)
    ys = x_sample.reshape(bs * ts, d)
    cv_p, cv_s = [], []
    slabs_p = slabs_s = None
    for l in range(n_layers):
        def mla(x, cos_t, sin_t, tm, slabs):
            return _mla_proj(x, row(mix_norm), w_small, row(q_a_norm), w_uq_p, row(q_nope_norm), g_qr2,
                             row(kv_a_norm), g_kr2, cos_t, sin_t, l, tm, n_layers, slabs)

        yp = _ffn(yp, row(ffn1_norm), w1g, w1u, w1d, l)
        q, lat_all_p, kr_all_p, nkt = mla(yp, cos_p, sin_p, tm_p, slabs_p)
        slabs_p = (lat_all_p, kr_all_p)
        kt, v = _kv_proj(lat_all_p.reshape(n_layers, bp, tp, KV_LORA), l,
                         jnp.transpose(nkt.reshape(QK_ROPE, bp, tp), (1, 0, 2)), w_uk_t, w_uv, g_kn_col, l)
        attn = _attention(q, kt, v, t=tp, past=0, n_valid=tp)
        yp, nc = _mix(yp, row(mix_norm), attn, w5, conv_w, zero_conv, w_o_b, l, tp, tm_p)
        yp = _cross(yp, row(cross_norm), w_cq_b, row(cq_norm), mem_k_p, mem_v_p, w_co_b, l, tp, tm_p, l)
        yp = _ffn(yp, row(ffn2_norm), w2g, w2u, w2d, l)
        cv_p.append(nc)

        ys = _ffn(ys, row(ffn1_norm), w1g, w1u, w1d, l)
        q, lat_all_s, kr_all_s, nkt = mla(ys, cos_s, sin_s, tm_s, slabs_s)
        slabs_s = (lat_all_s, kr_all_s)
        lat_keys = lax.dynamic_update_slice(lat_keys, lat_all_s[l].reshape(1, bs, ts, KV_LORA), (l, 0, past, 0))
        krt_keys = lax.dynamic_update_slice(
            krt_keys, jnp.transpose(nkt.reshape(1, QK_ROPE, bs, ts), (0, 2, 1, 3)), (l, 0, 0, past))
        attn = _kv_attention(q, lat_keys, krt_keys, w_uk_t, w_uv, g_kn_col, l, t=ts, past=past, n_valid=past + ts)
        ys, nc = _mix(ys, row(mix_norm), attn, w5, conv_w, state_conv[l], w_o_b, l, ts, tm_s)
        ys = _cross(ys, row(cross_norm), w_cq_b, row(cq_norm), mem_k_s, mem_v_s, w_co_b, l, ts, tm_s, l)
        ys = _ffn(ys, row(ffn2_norm), w2g, w2u, w2d, l)
        cv_s.append(nc)

    mem_shape = (n_layers, bp, n_mem, MEM_HEADS, MEM_HEAD_DIM)
    return (yp.reshape(bp, tp, d), ys.reshape(bs, ts, d),
            slabs_p[0].reshape(n_layers, bp, tp, KV_LORA), slabs_p[1].reshape(n_layers, bp, tp, QK_ROPE),
            jnp.stack(cv_p), mem_k_p.reshape(mem_shape), mem_v_p.reshape(mem_shape),
            slabs_s[0].reshape(n_layers, bs, ts, KV_LORA), slabs_s[1].reshape(n_layers, bs, ts, QK_ROPE),
            jnp.stack(cv_s))
```

```python
import functools
import math

import jax
import jax.numpy as jnp
from jax import lax
from jax.experimental import pallas as pl
from jax.experimental.pallas import tpu as pltpu

F32 = jnp.float32
BF16 = jnp.bfloat16

V_HEAD = 128
QK_NOPE = 128
QK_ROPE = 64
ROPE_HALF = QK_ROPE // 2
QK_DIM = QK_NOPE + QK_ROPE
Q_LORA = 768
KV_LORA = 512
CHUNK = 64
ROPE_BASE = 10000.0
MEM_HEADS = 4
MEM_HEAD_DIM = 128
CONV_WIDTH = 3
EPS = 1e-6
MLA_SCALE = 1.0 / math.sqrt(QK_NOPE + QK_ROPE)
MEM_SCALE = 1.0 / math.sqrt(MEM_HEAD_DIM)
LOG2_E = math.log2(math.e)
Q_SCALE = MLA_SCALE * LOG2_E

LANE = 128
V7X_VMEM_BYTES = 64 * 1024 * 1024
VMEM_LIMIT_BYTES = V7X_VMEM_BYTES * 13 // 16
LARGE_VMEM_LIMIT_BYTES = V7X_VMEM_BYTES * 15 // 16

FFN_TILE = 512
FFN_ROWS = 1024
FFN_ROW_CHUNK = 512
MIX_TILE = 512
ATTN_HEADS_PER_STEP = 2
KV_CHUNK = 256
N_SMALL = Q_LORA + KV_LORA + QK_ROPE
SMALL_W = Q_LORA + KV_LORA + LANE


def _params(*sem, vmem_limit_bytes=VMEM_LIMIT_BYTES):
    return pltpu.CompilerParams(dimension_semantics=sem, vmem_limit_bytes=vmem_limit_bytes)


def _tile(n, target):
    t = min(n, target)
    while n % t:
        t -= 1
    return t


def _rms(x, g):
    ms = jnp.mean(x * x, axis=-1, keepdims=True)
    return x * lax.rsqrt(ms + EPS) * g


def _sigmoid(x):
    return 1.0 / (1.0 + jnp.exp(-x))


def _dot(a, b):
    return jnp.dot(a, b, preferred_element_type=F32)


def _dot_nt(a, b):
    return lax.dot_general(a, b, (((1,), (1,)), ((), ())), preferred_element_type=F32)


def _ffn_body(x_ref, g_ref, wg_ref, wu_ref, wd_ref, o_ref, h_ref):
    tm = h_ref.shape[0]
    cm = _tile(tm, FFN_ROW_CHUNK)

    def down(h):
        gate = _dot(h, wg_ref[...])
        up = _dot(h, wu_ref[...])
        act = (gate * _sigmoid(gate) * up * 0.5).astype(BF16)
        return _dot(act, wd_ref[...])

    @pl.when(pl.program_id(1) == 0)
    def _():
        for r0 in range(0, tm, cm):
            x = x_ref[r0:r0 + cm, :]
            h = _rms(x, g_ref[...]).astype(BF16)
            h_ref[r0:r0 + cm, :] = h
            o_ref[r0:r0 + cm, :] = x + down(h)

    @pl.when(pl.program_id(1) != 0)
    def _():
        for r0 in range(0, tm, cm):
            o_ref[r0:r0 + cm, :] += down(h_ref[r0:r0 + cm, :])


def _ffn(x, g, wg, wu, wd, layer):
    n, d = x.shape
    f = wg.shape[-1]
    tm, tf = _tile(n, FFN_ROWS), _tile(f, FFN_TILE)
    return pl.pallas_call(
        _ffn_body,
        grid=(n // tm, f // tf),
        in_specs=[
            pl.BlockSpec((tm, d), lambda i, j: (i, 0)),
            pl.BlockSpec((None, 1, d), lambda i, j: (layer, 0, 0)),
            pl.BlockSpec((None, d, tf), lambda i, j: (layer, 0, j)),
            pl.BlockSpec((None, d, tf), lambda i, j: (layer, 0, j)),
            pl.BlockSpec((None, tf, d), lambda i, j: (layer, j, 0)),
        ],
        out_specs=pl.BlockSpec((tm, d), lambda i, j: (i, 0)),
        out_shape=jax.ShapeDtypeStruct((n, d), F32),
        scratch_shapes=[pltpu.VMEM((tm, d), BF16)],
        compiler_params=_params("parallel", "arbitrary", vmem_limit_bytes=LARGE_VMEM_LIMIT_BYTES),
        name="ffn",
    )(x, g, wg, wu, wd)


def _mla_proj_body(x_ref, gmix_ref, ws_ref, gqa_ref, wuq_ref, gqn_ref, gqr_ref, gkva_ref, gkr_ref,
                   cos_ref, sin_ref, *rest, n_heads, row_chunk):
    q_ref, lat_ref, kr_ref, krt_ref = rest[-4:]
    tm = x_ref.shape[0]
    cm = _tile(tm, row_chunk)
    lane = lax.broadcasted_iota(jnp.int32, (cm, LANE), 1)
    first_half = (lane & ROPE_HALF) == 0
    low_group = lane < QK_ROPE
    gqn = gqn_ref[...]
    gqr = gqr_ref[...]
    rope_base = n_heads * QK_NOPE

    for r0 in range(0, tm, cm):
        rs = slice(r0, r0 + cm)
        h = _rms(x_ref[rs, :], gmix_ref[...]).astype(BF16)
        u = _dot(h, ws_ref[...])
        c_q = u[:, :Q_LORA]
        c_kv = u[:, Q_LORA:Q_LORA + KV_LORA]
        k_r = u[:, Q_LORA + KV_LORA:]

        lat_ref[rs, :] = _rms(c_kv, gkva_ref[...])

        cos_t = cos_ref[rs, :]
        sin_t = sin_ref[rs, :]

        def rope(y):
            swapped = jnp.where(first_half, pltpu.roll(y, LANE - ROPE_HALF, 1), pltpu.roll(y, ROPE_HALF, 1))
            return y * cos_t + swapped * sin_t

        ms = jnp.sum(k_r * k_r, axis=-1, keepdims=True) * (1.0 / QK_ROPE)
        kr = rope(k_r * lax.rsqrt(ms + EPS) * gkr_ref[...])
        kr_ref[rs, :] = kr[:, :QK_ROPE]
        krt_ref[:, rs] = kr.T[:QK_ROPE, :].astype(BF16)

        cq_n = _rms(c_q, gqa_ref[...]).astype(BF16)
        q = _dot(cq_n, wuq_ref[...])
        for hd in range(n_heads):
            qn = _rms(q[:, hd * QK_NOPE:(hd + 1) * QK_NOPE], gqn) * Q_SCALE
            q_ref[hd, rs, :QK_NOPE] = qn.astype(BF16)
        for pair in range(n_heads // 2):
            col = q[:, rope_base + pair * LANE: rope_base + (pair + 1) * LANE]
            sq = col * col
            lo = jnp.sum(jnp.where(low_group, sq, 0.0), axis=-1, keepdims=True)
            hi = jnp.sum(jnp.where(low_group, 0.0, sq), axis=-1, keepdims=True)
            ms = jnp.where(low_group, lo, hi) * (1.0 / QK_ROPE)
            y = (rope(col * lax.rsqrt(ms + EPS) * gqr) * Q_SCALE).astype(BF16)
            q_ref[2 * pair, rs, QK_NOPE:] = y[:, :QK_ROPE]
            q_ref[2 * pair + 1, rs, QK_NOPE:] = y[:, QK_ROPE:]


def _mla_proj(x, gmix, ws, gqa, wuq, gqn, gqr2, gkva, gkr2, cos_t, sin_t, layer, tm, n_layers, slabs):
    n, d = x.shape
    alias_specs = [] if slabs is None else [pl.BlockSpec(memory_space=pl.ANY)] * 2
    n_in = 11
    n_heads = wuq.shape[-1] // QK_DIM
    n_pos_tiles = cos_t.shape[0] // tm
    const = dict(pipeline_mode=pl.Buffered(1))
    return pl.pallas_call(
        functools.partial(_mla_proj_body, n_heads=n_heads, row_chunk=256),
        grid=(n // tm,),
        in_specs=[
            pl.BlockSpec((tm, d), lambda i: (i, 0)),
            pl.BlockSpec((None, 1, d), lambda i: (layer, 0, 0)),
            pl.BlockSpec((None, d, SMALL_W), lambda i: (layer, 0, 0), **const),
            pl.BlockSpec((None, 1, Q_LORA), lambda i: (layer, 0, 0)),
            pl.BlockSpec((None, Q_LORA, n_heads * QK_DIM), lambda i: (layer, 0, 0), **const),
            pl.BlockSpec((None, 1, QK_NOPE), lambda i: (layer, 0, 0)),
            pl.BlockSpec((None, 1, LANE), lambda i: (layer, 0, 0)),
            pl.BlockSpec((None, 1, KV_LORA), lambda i: (layer, 0, 0)),
            pl.BlockSpec((None, 1, LANE), lambda i: (layer, 0, 0)),
            pl.BlockSpec((tm, LANE), lambda i: (i % n_pos_tiles, 0)),
            pl.BlockSpec((tm, LANE), lambda i: (i % n_pos_tiles, 0)),
            *alias_specs,
        ],
        out_specs=[
            pl.BlockSpec((n_heads, tm, QK_DIM), lambda i: (0, i, 0)),
            pl.BlockSpec((None, tm, KV_LORA), lambda i: (layer, i, 0)),
            pl.BlockSpec((None, tm, QK_ROPE), lambda i: (layer, i, 0)),
            pl.BlockSpec((QK_ROPE, tm), lambda i: (0, i)),
        ],
        out_shape=[
            jax.ShapeDtypeStruct((n_heads, n, QK_DIM), BF16),
            jax.ShapeDtypeStruct((n_layers, n, KV_LORA), F32),
            jax.ShapeDtypeStruct((n_layers, n, QK_ROPE), F32),
            jax.ShapeDtypeStruct((QK_ROPE, n), BF16),
        ],
        input_output_aliases={} if slabs is None else {n_in: 1, n_in + 1: 2},
        compiler_params=_params("parallel"),
        name="mla_proj",
    )(x, gmix, ws, gqa, wuq, gqn, gqr2, gkva, gkr2, cos_t, sin_t, *(slabs or ()))


def _kv_proj_body(lat_ref, krt_ref, wukt_ref, wuv_ref, gkn_ref, kt_ref, v_ref, *, n_heads):
    tm = lat_ref.shape[0]
    cm = _tile(tm, KV_CHUNK)
    gkn = gkn_ref[...]
    for r0 in range(0, tm, cm):
        lat = lat_ref[r0:r0 + cm, :].astype(BF16)
        knt = _dot_nt(wukt_ref[...], lat)
        krt = krt_ref[:, r0:r0 + cm]
        for hd in range(n_heads):
            blk = knt[hd * QK_NOPE:(hd + 1) * QK_NOPE, :]
            ms = jnp.mean(blk * blk, axis=0, keepdims=True)
            kt_ref[hd, :QK_NOPE, r0:r0 + cm] = (blk * lax.rsqrt(ms + EPS) * gkn).astype(BF16)
            kt_ref[hd, QK_NOPE:, r0:r0 + cm] = krt
        v = _dot(lat, wuv_ref[...])
        for hd in range(n_heads):
            v_ref[hd, r0:r0 + cm, :] = v[:, hd * V_HEAD:(hd + 1) * V_HEAD].astype(BF16)


def _kv_proj(lat, lat_layer, krt, wukt, wuv, gkn_col, layer):
    _, b, tk, _ = lat.shape
    n_heads = wuv.shape[-1] // V_HEAD
    tm = _tile(tk, 512)
    const = dict(pipeline_mode=pl.Buffered(1))
    return pl.pallas_call(
        functools.partial(_kv_proj_body, n_heads=n_heads),
        grid=(b, tk // tm),
        in_specs=[
            pl.BlockSpec((None, None, tm, KV_LORA), lambda i, j: (lat_layer, i, j, 0)),
            pl.BlockSpec((None, QK_ROPE, tm), lambda i, j: (i, 0, j)),
            pl.BlockSpec((None, n_heads * QK_NOPE, KV_LORA), lambda i, j: (layer, 0, 0), **const),
            pl.BlockSpec((None, KV_LORA, n_heads * V_HEAD), lambda i, j: (layer, 0, 0), **const),
            pl.BlockSpec((None, QK_NOPE, 1), lambda i, j: (layer, 0, 0)),
        ],
        out_specs=[
            pl.BlockSpec((None, n_heads, QK_DIM, tm), lambda i, j: (i, 0, 0, j)),
            pl.BlockSpec((None, n_heads, tm, V_HEAD), lambda i, j: (i, 0, j, 0)),
        ],
        out_shape=[
            jax.ShapeDtypeStruct((b, n_heads, QK_DIM, tk), BF16),
            jax.ShapeDtypeStruct((b, n_heads, tk, V_HEAD), BF16),
        ],
        compiler_params=_params("parallel", "parallel"),
        name="kv_proj",
    )(lat, krt, wukt, wuv, gkn_col)


def _attn_body(q_ref, kt_ref, v_ref, o_ref, *, t, tq, past, n_valid, key_tile):
    n_h, _, tk = kt_ref.shape
    blocks = []
    for hd in range(n_h):
        for qs in range(0, t, tq):
            c_lo = (past + qs) // CHUNK
            c_hi = (past + qs + tq - 1) // CHUNK
            full_end = min((c_lo + 1) * CHUNK, n_valid) // LANE * LANE
            edge_end = min(-(-min((c_hi + 1) * CHUNK, n_valid) // LANE) * LANE, tk)
            tiles = [(lo, min(lo + key_tile, full_end), False) for lo in range(0, full_end, key_tile)]
            tiles += [(lo, min(lo + key_tile, edge_end), True) for lo in range(full_end, edge_end, key_tile)]
            blocks.append((hd, qs, tiles))

    def scores(hd, qs, tile):
        lo, hi, needs_mask = tile
        s = _dot(q_ref[hd, qs:qs + tq, :], kt_ref[hd, :, lo:hi])
        if needs_mask:
            q_chunk = (lax.broadcasted_iota(jnp.int32, (tq, hi - lo), 0) + (past + qs)) >> 6
            k_pos = lax.broadcasted_iota(jnp.int32, (tq, hi - lo), 1) + lo
            s = jnp.where(jnp.logical_and((k_pos >> 6) <= q_chunk, k_pos < n_valid), s, -jnp.inf)
        return s

    def lane_groups(a):
        return [a[:, c:c + LANE] for c in range(0, a.shape[1], LANE)]

    cur = [scores(blocks[0][0], blocks[0][1], tile) for tile in blocks[0][2]]
    for bi, (hd, qs, tiles) in enumerate(blocks):
        nxt_hd, nxt_qs, nxt_tiles = blocks[bi + 1] if bi + 1 < len(blocks) else (None, None, [])
        m = jnp.max(functools.reduce(jnp.maximum, [g for s in cur for g in lane_groups(s)]),
                    axis=-1, keepdims=True)
        nxt = []
        l_acc = 0.0
        ps = []
        for k, s in enumerate(cur):
            upto = len(nxt_tiles) if k == len(cur) - 1 else (k + 1) * len(nxt_tiles) // len(cur)
            for tile in nxt_tiles[len(nxt):upto]:
                nxt.append(scores(nxt_hd, nxt_qs, tile))
            p = jnp.exp2(s - m)
            l_acc = l_acc + functools.reduce(jnp.add, lane_groups(p))
            ps.append(p.astype(BF16))
        p_all = ps[0] if len(ps) == 1 else jnp.concatenate(ps, axis=1)
        o = _dot(p_all, v_ref[hd, tiles[0][0]:tiles[-1][1], :])
        o_ref[qs:qs + tq, hd * V_HEAD:(hd + 1) * V_HEAD] = o * (1.0 / jnp.sum(l_acc, axis=-1, keepdims=True))
        cur = nxt


def _attention(q, kt, v, *, t, past, n_valid):
    assert CHUNK == 64
    n_heads, n, _ = q.shape
    b, _, _, tk = kt.shape
    tq = _tile(t, 256)
    hps = _tile(n_heads, ATTN_HEADS_PER_STEP)
    return pl.pallas_call(
        functools.partial(_attn_body, t=t, tq=tq, past=past, n_valid=n_valid, key_tile=256),
        grid=(b, n_heads // hps),
        in_specs=[
            pl.BlockSpec((hps, t, QK_DIM), lambda i, j: (j, i, 0)),
            pl.BlockSpec((None, hps, QK_DIM, tk), lambda i, j: (i, j, 0, 0)),
            pl.BlockSpec((None, hps, tk, V_HEAD), lambda i, j: (i, j, 0, 0)),
        ],
        out_specs=pl.BlockSpec((t, hps * V_HEAD), lambda i, j: (i, j)),
        out_shape=jax.ShapeDtypeStruct((n, n_heads * V_HEAD), F32),
        compiler_params=_params("parallel", "parallel"),
        name="attention",
    )(q, kt, v)


def _kv_attn_body(q_ref, lat_ref, krt_ref, wukt_ref, wuv_ref, gkn_ref, o_ref, kt_ref, v_ref, *, n_heads, t, past, n_valid):
    _kv_proj_body(lat_ref, krt_ref, wukt_ref, wuv_ref, gkn_ref, kt_ref, v_ref, n_heads=n_heads)
    _attn_body(q_ref, kt_ref, v_ref, o_ref, t=t, tq=t, past=past, n_valid=n_valid, key_tile=256)


def _kv_attention(q, lat, krt, wukt, wuv, gkn_col, layer, *, t, past, n_valid):
    n_heads, n, _ = q.shape
    _, b, tk, _ = lat.shape
    const = dict(pipeline_mode=pl.Buffered(1))
    return pl.pallas_call(
        functools.partial(_kv_attn_body, n_heads=n_heads, t=t, past=past, n_valid=n_valid),
        grid=(b,),
        in_specs=[
            pl.BlockSpec((n_heads, t, QK_DIM), lambda i: (0, i, 0)),
            pl.BlockSpec((None, None, tk, KV_LORA), lambda i: (layer, i, 0, 0)),
            pl.BlockSpec((None, None, QK_ROPE, tk), lambda i: (layer, i, 0, 0)),
            pl.BlockSpec((None, n_heads * QK_NOPE, KV_LORA), lambda i: (layer, 0, 0), **const),
            pl.BlockSpec((None, KV_LORA, n_heads * V_HEAD), lambda i: (layer, 0, 0), **const),
            pl.BlockSpec((None, QK_NOPE, 1), lambda i: (layer, 0, 0)),
        ],
        out_specs=pl.BlockSpec((t, n_heads * V_HEAD), lambda i: (i, 0)),
        out_shape=jax.ShapeDtypeStruct((n, n_heads * V_HEAD), F32),
        scratch_shapes=[pltpu.VMEM((n_heads, QK_DIM, tk), BF16), pltpu.VMEM((n_heads, tk, V_HEAD), BF16)],
        compiler_params=_params("arbitrary", vmem_limit_bytes=LARGE_VMEM_LIMIT_BYTES),
        name="kv_attention",
    )(q, lat, krt, wukt, wuv, gkn_col)


def _mix_body(x_ref, g_ref, a_ref, wb_ref, wc_ref, wx_ref, wga_ref, wgc_ref, cw_ref, prev_ref, wo_ref,
              o_ref, nc_ref, h_ref, carry_ref, *, t_seq, tiles_per_seq, row_chunk):
    tm, tn = a_ref.shape
    i = pl.program_id(0)
    j = pl.program_id(1)

    cw = cw_ref[...]
    if tiles_per_seq == 1:
        first_befores = [(s * t_seq, prev_ref[s, 0:1, :], prev_ref[s, 1:2, :]) for s in range(tm // t_seq)]
        cm = tm
    else:
        @pl.when(i % tiles_per_seq == 0)
        def _():
            carry_ref[j] = prev_ref[0]

        first_befores = [(0, carry_ref[j, 0:1, :], carry_ref[j, 1:2, :])]
        cm = _tile(tm, row_chunk)

    def run(first_col_tile):
        befores = first_befores
        for r0 in range(0, tm, cm):
            if first_col_tile:
                x = x_ref[r0:r0 + cm, :]
                h = _rms(x, g_ref[...]).astype(BF16)
                h_ref[r0:r0 + cm, :] = h
            else:
                h = h_ref[r0:r0 + cm, :]
            z = _dot(h, wc_ref[...]) * _dot(h, wx_ref[...])
            rows = lax.broadcasted_iota(jnp.int32, (cm, tn), 0)
            z1 = pltpu.roll(z, 1, 0)
            z2 = pltpu.roll(z, 2, 0)
            for row, p0, p1 in befores:
                z1 = jnp.where(rows == row, p1, z1)
                z2 = jnp.where(rows == row, p0, jnp.where(rows == row + 1, p1, z2))
            befores = [(0, z[cm - 2:cm - 1, :], z[cm - 1:, :])]
            y = cw[0:1, :] * z2 + cw[1:2, :] * z1 + cw[2:3, :] * z
            conv_out = _dot(h, wb_ref[...]) * y
            mixed = (_sigmoid(_dot(h, wga_ref[...])) * a_ref[r0:r0 + cm, :]
                     + _sigmoid(_dot(h, wgc_ref[...])) * conv_out)
            out = _dot(mixed.astype(BF16), wo_ref[...])
            if first_col_tile:
                o_ref[r0:r0 + cm, :] = x + out
            else:
                o_ref[r0:r0 + cm, :] += out
            if tiles_per_seq == 1:
                for s in range(tm // t_seq):
                    nc_ref[s] = z[(s + 1) * t_seq - 2:(s + 1) * t_seq, :]
            elif r0 + cm == tm:
                carry_ref[j] = z[cm - 2:, :]
                nc_ref[0] = z[cm - 2:, :]

    pl.when(j == 0)(functools.partial(run, True))
    pl.when(j != 0)(functools.partial(run, False))


def _mix(x, g, attn, w5, cw, prev, wo, layer, t_seq, tm):
    n, d = x.shape
    tn = _tile(d, MIX_TILE)
    assert CONV_WIDTH == 3 and t_seq >= CONV_WIDTH - 1
    if tm >= t_seq:
        assert tm % t_seq == 0
        n_seq, tiles_per_seq = tm // t_seq, 1
        prev_map = lambda i, j: (i, 0, j)
    else:
        assert t_seq % tm == 0
        n_seq, tiles_per_seq = 1, t_seq // tm
        prev_map = lambda i, j: (i // tiles_per_seq, 0, j)
    w_specs = [pl.BlockSpec((None, d, tn), lambda i, j, c0=k * d // tn: (layer, 0, c0 + j)) for k in range(5)]
    y, tail_rows = pl.pallas_call(
        functools.partial(_mix_body, t_seq=t_seq, tiles_per_seq=tiles_per_seq, row_chunk=256),
        grid=(n // tm, d // tn),
        in_specs=[
            pl.BlockSpec((tm, d), lambda i, j: (i, 0)),
            pl.BlockSpec((None, 1, d), lambda i, j: (layer, 0, 0)),
            pl.BlockSpec((tm, tn), lambda i, j: (i, j)),
            *w_specs,
            pl.BlockSpec((None, CONV_WIDTH, tn), lambda i, j: (layer, 0, j)),
            pl.BlockSpec((n_seq, CONV_WIDTH - 1, tn), prev_map),
            pl.BlockSpec((None, tn, d), lambda i, j: (layer, j, 0)),
        ],
        out_specs=[
            pl.BlockSpec((tm, d), lambda i, j: (i, 0)),
            pl.BlockSpec((n_seq, CONV_WIDTH - 1, tn), lambda i, j: (i, 0, j)),
        ],
        out_shape=[
            jax.ShapeDtypeStruct((n, d), F32),
            jax.ShapeDtypeStruct((n // tm * n_seq, CONV_WIDTH - 1, d), F32),
        ],
        scratch_shapes=[pltpu.VMEM((tm, d), BF16), pltpu.VMEM((d // tn, CONV_WIDTH - 1, tn), F32)],
        compiler_params=_params("arbitrary", "arbitrary"),
        name="mix",
    )(x, g, attn, w5, w5, w5, w5, w5, cw, prev, wo)
    return y, tail_rows.reshape(-1, tiles_per_seq, CONV_WIDTH - 1, d)[:, -1]


def _cross_body(x_ref, g_ref, wq_ref, gq_ref, mk_ref, mv_ref, wo_ref, o_ref, *, t_seq):
    tm = x_ref.shape[0]
    gq = gq_ref[...]
    cm = min(t_seq, tm)
    for r0 in range(0, tm, cm):
        s = r0 // cm
        x = x_ref[r0:r0 + cm, :]
        h = _rms(x, g_ref[...]).astype(BF16)
        q = _dot(h, wq_ref[...])
        heads = []
        for hd in range(MEM_HEADS):
            c0 = hd * MEM_HEAD_DIM
            qh = (_rms(q[:, c0:c0 + MEM_HEAD_DIM], gq) * (MEM_SCALE * LOG2_E)).astype(BF16)
            kh = mk_ref[s, :, c0:c0 + MEM_HEAD_DIM].astype(BF16)
            vh = mv_ref[s, :, c0:c0 + MEM_HEAD_DIM].astype(BF16)
            sc = _dot_nt(qh, kh)
            p = jnp.exp2(sc - jnp.max(sc, axis=-1, keepdims=True))
            l = jnp.sum(p, axis=-1, keepdims=True)
            heads.append((_dot(p.astype(BF16), vh) * (1.0 / l)).astype(BF16))
        o_ref[r0:r0 + cm, :] = x + _dot(jnp.concatenate(heads, axis=1), wo_ref[...])


def _cross(x, g, wq, gq, mem_k, mem_v, wo, layer, t_seq, tm, mem_layer):
    n, d = x.shape
    _, _, n_mem, dm = mem_k.shape
    if tm >= t_seq:
        n_seq = tm // t_seq
        mem_map = lambda i: (mem_layer, i, 0, 0)
    else:
        n_seq, tiles_per_seq = 1, t_seq // tm
        mem_map = lambda i: (mem_layer, i // tiles_per_seq, 0, 0)
    const = dict(pipeline_mode=pl.Buffered(1))
    return pl.pallas_call(
        functools.partial(_cross_body, t_seq=t_seq),
        grid=(n // tm,),
        in_specs=[
            pl.BlockSpec((tm, d), lambda i: (i, 0)),
            pl.BlockSpec((None, 1, d), lambda i: (layer, 0, 0)),
            pl.BlockSpec((None, d, dm), lambda i: (layer, 0, 0), **const),
            pl.BlockSpec((None, 1, MEM_HEAD_DIM), lambda i: (layer, 0, 0)),
            pl.BlockSpec((None, n_seq, n_mem, dm), mem_map),
            pl.BlockSpec((None, n_seq, n_mem, dm), mem_map),
            pl.BlockSpec((None, dm, d), lambda i: (layer, 0, 0), **const),
        ],
        out_specs=pl.BlockSpec((tm, d), lambda i: (i, 0)),
        out_shape=jax.ShapeDtypeStruct((n, d), F32),
        compiler_params=_params("parallel"),
        name="cross",
    )(x, g, wq, gq, mem_k, mem_v, wo)


def _mem_kv_body(m_ref, g_ref, wk_ref, wv_ref, gk_ref, k_ref, v_ref):
    m = _rms(m_ref[...], g_ref[...]).astype(BF16)
    k = _dot(m, wk_ref[...])
    gk = gk_ref[...]
    for hd in range(MEM_HEADS):
        c0 = hd * MEM_HEAD_DIM
        k_ref[:, c0:c0 + MEM_HEAD_DIM] = _rms(k[:, c0:c0 + MEM_HEAD_DIM], gk)
    v_ref[...] = _dot(m, wv_ref[...])


def _mem_kv(mem, g, wk, wv, gk):
    n, d = mem.shape
    n_layers, _, dm = wk.shape
    tm = _tile(n, 512)
    out = jax.ShapeDtypeStruct((n_layers, n, dm), F32)
    return pl.pallas_call(
        _mem_kv_body,
        grid=(n_layers, n // tm),
        in_specs=[
            pl.BlockSpec((tm, d), lambda l, i: (i, 0)),
            pl.BlockSpec((None, 1, d), lambda l, i: (l, 0, 0)),
            pl.BlockSpec((None, d, dm), lambda l, i: (l, 0, 0)),
            pl.BlockSpec((None, d, dm), lambda l, i: (l, 0, 0)),
            pl.BlockSpec((None, 1, MEM_HEAD_DIM), lambda l, i: (l, 0, 0)),
        ],
        out_specs=[
            pl.BlockSpec((None, tm, dm), lambda l, i: (l, i, 0)),
            pl.BlockSpec((None, tm, dm), lambda l, i: (l, i, 0)),
        ],
        out_shape=[out, out],
        compiler_params=_params("parallel", "parallel"),
        name="mem_kv",
    )(mem, g, wk, wv, gk)


def _rope_tables(pos, reps):
    inv = ROPE_BASE ** (-jnp.arange(ROPE_HALF, dtype=F32) / ROPE_HALF)
    ang = pos.astype(F32)[:, None] * inv[None, :]
    cos, sin = jnp.cos(ang), jnp.sin(ang)
    groups = LANE // QK_ROPE
    cos_t = jnp.tile(jnp.concatenate([cos, cos], axis=1), (reps, groups))
    sin_t = jnp.tile(jnp.concatenate([-sin, sin], axis=1), (reps, groups))
    return cos_t, sin_t


def kernel(x_prompt, x_sample, mem_prompt, cache_kv_latent, cache_k_rope, state_conv, cache_mem_k, cache_mem_v, ffn1_norm, ffn1_w_gate, ffn1_w_up, ffn1_w_down, mix_norm, w_in, q_a_norm, w_uq, q_nope_norm, q_rope_norm, kv_a_norm, w_ukv, k_nope_norm, k_rope_norm, conv_w, w_o, cross_norm, mem_norm, w_cq, w_ck, w_cv, cq_norm, ck_norm, w_co, ffn2_norm, ffn2_w_gate, ffn2_w_up, ffn2_w_down):
    bp, tp, d = x_prompt.shape
    bs, ts, _ = x_sample.shape
    n_layers = w_in.shape[0]
    past = cache_kv_latent.shape[2]
    n_mem = mem_prompt.shape[1]
    n_heads = w_uq.shape[-1] // QK_DIM
    dm = MEM_HEADS * MEM_HEAD_DIM

    row = lambda a: a[:, None, :]
    w1g, w1u, w1d = ffn1_w_gate.astype(BF16), ffn1_w_up.astype(BF16), ffn1_w_down.astype(BF16)
    w2g, w2u, w2d = ffn2_w_gate.astype(BF16), ffn2_w_up.astype(BF16), ffn2_w_down.astype(BF16)
    w_small = jnp.pad(w_in[:, :, :N_SMALL], ((0, 0), (0, 0), (0, SMALL_W - N_SMALL))).astype(BF16)
    w5 = w_in[:, :, N_SMALL:].astype(BF16)
    uq = w_uq.reshape(n_layers, Q_LORA, n_heads, QK_DIM)
    w_uq_p = jnp.concatenate([uq[..., :QK_NOPE].reshape(n_layers, Q_LORA, n_heads * QK_NOPE),
                              uq[..., QK_NOPE:].reshape(n_layers, Q_LORA, n_heads * QK_ROPE)], axis=-1).astype(BF16)
    ukv = w_ukv.reshape(n_layers, KV_LORA, n_heads, QK_NOPE + V_HEAD)
    w_uk_t = jnp.transpose(ukv[..., :QK_NOPE], (0, 2, 3, 1)).reshape(n_layers, n_heads * QK_NOPE, KV_LORA).astype(BF16)
    w_uv = ukv[..., QK_NOPE:].reshape(n_layers, KV_LORA, n_heads * V_HEAD).astype(BF16)
    w_o_b = w_o.astype(BF16)
    w_cq_b, w_ck_b, w_cv_b, w_co_b = w_cq.astype(BF16), w_ck.astype(BF16), w_cv.astype(BF16), w_co.astype(BF16)
    g_qr2 = row(jnp.tile(q_rope_norm, (1, LANE // QK_ROPE)))
    g_kr2 = row(jnp.tile(k_rope_norm, (1, LANE // QK_ROPE)))
    g_kn_col = k_nope_norm[:, :, None]

    tm_p = _tile(tp, 512)
    tm_s = bs * ts
    cos_p, sin_p = _rope_tables(jnp.arange(tp, dtype=jnp.int32), 1)
    cos_s, sin_s = _rope_tables(past + jnp.arange(ts, dtype=jnp.int32), bs)

    tk_s = -(-(past + ts) // KV_CHUNK) * KV_CHUNK
    lat_keys = jnp.pad(cache_kv_latent, ((0, 0), (0, 0), (0, tk_s - past), (0, 0)))
    krt_keys = jnp.pad(jnp.swapaxes(cache_k_rope, -1, -2).astype(BF16), ((0, 0), (0, 0), (0, 0), (0, tk_s - past)))

    mem_k_p, mem_v_p = _mem_kv(mem_prompt.reshape(bp * n_mem, d), row(mem_norm), w_ck_b, w_cv_b, row(ck_norm))
    mem_k_p = mem_k_p.reshape(n_layers, bp, n_mem, dm)
    mem_v_p = mem_v_p.reshape(n_layers, bp, n_mem, dm)
    mem_k_s = cache_mem_k.reshape(n_layers, bs, n_mem, dm)
    mem_v_s = cache_mem_v.reshape(n_layers, bs, n_mem, dm)
    zero_conv = jnp.zeros((bp, CONV_WIDTH - 1, d), F32)

    yp = x_prompt.reshape(bp * tp, d)
    ys = x_sample.reshape(bs * ts, d)
    cv_p, cv_s = [], []
    slabs_p = slabs_s = None
    for l in range(n_layers):
        def mla(x, cos_t, sin_t, tm, slabs):
            return _mla_proj(x, row(mix_norm), w_small, row(q_a_norm), w_uq_p, row(q_nope_norm), g_qr2,
                             row(kv_a_norm), g_kr2, cos_t, sin_t, l, tm, n_layers, slabs)

        yp = _ffn(yp, row(ffn1_norm), w1g, w1u, w1d, l)
        q, lat_all_p, kr_all_p, nkt = mla(yp, cos_p, sin_p, tm_p, slabs_p)
        slabs_p = (lat_all_p, kr_all_p)
        kt, v = _kv_proj(lat_all_p.reshape(n_layers, bp, tp, KV_LORA), l,
                         jnp.transpose(nkt.reshape(QK_ROPE, bp, tp), (1, 0, 2)), w_uk_t, w_uv, g_kn_col, l)
        attn = _attention(q, kt, v, t=tp, past=0, n_valid=tp)
        yp, nc = _mix(yp, row(mix_norm), attn, w5, conv_w, zero_conv, w_o_b, l, tp, tm_p)
        yp = _cross(yp, row(cross_norm), w_cq_b, row(cq_norm), mem_k_p, mem_v_p, w_co_b, l, tp, tm_p, l)
        yp = _ffn(yp, row(ffn2_norm), w2g, w2u, w2d, l)
        cv_p.append(nc)

        ys = _ffn(ys, row(ffn1_norm), w1g, w1u, w1d, l)
        q, lat_all_s, kr_all_s, nkt = mla(ys, cos_s, sin_s, tm_s, slabs_s)
        slabs_s = (lat_all_s, kr_all_s)
        lat_keys = lax.dynamic_update_slice(lat_keys, lat_all_s[l].reshape(1, bs, ts, KV_LORA), (l, 0, past, 0))
        krt_keys = lax.dynamic_update_slice(
            krt_keys, jnp.transpose(nkt.reshape(1, QK_ROPE, bs, ts), (0, 2, 1, 3)), (l, 0, 0, past))
        attn = _kv_attention(q, lat_keys, krt_keys, w_uk_t, w_uv, g_kn_col, l, t=ts, past=past, n_valid=past + ts)
        ys, nc = _mix(ys, row(mix_norm), attn, w5, conv_w, state_conv[l], w_o_b, l, ts, tm_s)
        ys = _cross(ys, row(cross_norm), w_cq_b, row(cq_norm), mem_k_s, mem_v_s, w_co_b, l, ts, tm_s, l)
        ys = _ffn(ys, row(ffn2_norm), w2g, w2u, w2d, l)
        cv_s.append(nc)

    mem_shape = (n_layers, bp, n_mem, MEM_HEADS, MEM_HEAD_DIM)
    return (yp.reshape(bp, tp, d), ys.reshape(bs, ts, d),
            slabs_p[0].reshape(n_layers, bp, tp, KV_LORA), slabs_p[1].reshape(n_layers, bp, tp, QK_ROPE),
            jnp.stack(cv_p), mem_k_p.reshape(mem_shape), mem_v_p.reshape(mem_shape),
            slabs_s[0].reshape(n_layers, bs, ts, KV_LORA), slabs_s[1].reshape(n_layers, bs, ts, QK_ROPE),
            jnp.stack(cv_s))
```
